```python
import jax, jax.numpy as jnp
from jax import lax
import numpy as np

D_MODEL = 1024
BATCH = 8
SEQ = 4096
DEPTH = 4

GRID_W = 64
CTX_LEN = 256
N_MIXERS = 3
N_LAYERS_A = (DEPTH + 2) // 3
N_LAYERS_B = (DEPTH + 1) // 3
N_LAYERS_C = DEPTH // 3
N_SUB = 3
N_MOD = 3 * N_SUB
D_FF = 256 * ((8 * D_MODEL // 3 + 255) // 256)
FFN_RES = 0.5
RET_DK = 256
RET_HEADS = D_MODEL // RET_DK
RET_DV = 2 * RET_DK
RET_CHUNK = 128
ROPE_BASE = 10000.0
NA_HEAD_DIM = 64
NA_HEADS = D_MODEL // NA_HEAD_DIM
NA_WIN_R = 8
NA_WIN_C = 16
NEG_INF = -1e30
LRU_WIDTH = D_MODEL
LRU_BLOCK_W = 256
LRU_BLOCKS = LRU_WIDTH // LRU_BLOCK_W
LRU_CONV_W = 4
LRU_C = 8.0
DEEPNORM_ALPHA = (2 * DEPTH) ** 0.25
DEEPNORM_BETA = (8 * DEPTH) ** -0.25
LN_EPS = 1e-5

kernel_name = 'hybrid_retention_natten_rglru_dit'


def layer_norm(x, g, b):
    xf = x.astype(jnp.float32)
    mu = jnp.mean(xf, axis=-1, keepdims=True)
    var = jnp.mean(jnp.square(xf - mu), axis=-1, keepdims=True)
    y = (xf - mu) * lax.rsqrt(var + LN_EPS) * g.astype(jnp.float32) + b.astype(jnp.float32)
    return y.astype(x.dtype)


def modulate(h, shift, scale):
    return h * (1.0 + scale) + shift


def post_norm(h, delta, g, b):
    return layer_norm(DEEPNORM_ALPHA * h + delta, g, b)


def swiglu(h, w_in, w_out):
    gate, up = jnp.split(h @ w_in, 2, axis=-1)
    return (jax.nn.silu(gate) * up) @ w_out


def half_ffn(h, shift, scale, gate, w_in, w_out, g, b):
    return post_norm(h, gate * (FFN_RES * swiglu(modulate(h, shift, scale), w_in, w_out)), g, b)


def retention_log_gammas(reverse):
    h = jnp.arange(RET_HEADS, dtype=jnp.float32)
    if reverse:
        h = h[::-1]
    return jnp.log1p(-jnp.exp2(-5.0 - h))


def axial_rope(t):
    n, dk = t.shape[2], t.shape[3]
    half = dk // 2
    pos = jnp.arange(n)
    freqs = ROPE_BASE ** (-jnp.arange(0, half, 2, dtype=jnp.float32) / half)

    def rot(u, p):
        ang = p.astype(jnp.float32)[:, None] * freqs
        cos, sin = jnp.cos(ang), jnp.sin(ang)
        u1, u2 = jnp.split(u, 2, axis=-1)
        return jnp.concatenate([u1 * cos - u2 * sin, u1 * sin + u2 * cos], axis=-1)

    return jnp.concatenate([rot(t[..., :half], pos // GRID_W), rot(t[..., half:], pos % GRID_W)], axis=-1)


def retention_scan(q, k, v, log_g, s0, inclusive):
    b, h, n, _ = q.shape
    nc = n // RET_CHUNK
    pos = jnp.arange(RET_CHUNK, dtype=jnp.float32)
    diff = pos[:, None] - pos[None, :]
    visible = diff >= 0 if inclusive else diff > 0
    intra = jnp.where(visible, jnp.exp(log_g[:, None, None] * jnp.maximum(diff, 0.0)), 0.0)
    q_dec = jnp.exp(log_g[:, None] * (pos + 1.0))[..., None]
    k_dec = jnp.exp(log_g[:, None] * (RET_CHUNK - 1.0 - pos))[..., None]
    chunk_dec = jnp.exp(log_g * RET_CHUNK)[:, None, None]

    def to_chunks(t):
        return jnp.moveaxis(t.reshape(b, h, nc, RET_CHUNK, t.shape[-1]), 2, 0)

    def step(state, blk):
        qb, kb, vb = blk
        att = jnp.einsum('bhid,bhjd->bhij', qb, kb) * intra
        o = jnp.einsum('bhij,bhjv->bhiv', att, vb) + jnp.einsum('bhid,bhdv->bhiv', qb * q_dec, state)
        state = state * chunk_dec + jnp.einsum('bhjd,bhjv->bhdv', kb * k_dec, vb)
        return state, o

    s_final, o = lax.scan(step, s0, (to_chunks(q), to_chunks(k), to_chunks(v)))
    o = jnp.moveaxis(o, 0, 2).reshape(b, h, n, v.shape[-1])
    return o, s_final


def retention_bidir(q, k, v, s0_f, s0_b):
    o_f, s_f = retention_scan(q, k, v, retention_log_gammas(False), s0_f, True)
    flip = lambda t: jnp.flip(t, axis=2)
    o_b, s_b = retention_scan(flip(q), flip(k), flip(v), retention_log_gammas(True), s0_b, False)
    return o_f + flip(o_b), s_f, s_b


def retention_mixer(h_lat, h_ctx, w_in, w_out, with_ctx):
    d_qk = RET_HEADS * RET_DK
    d_v = RET_HEADS * RET_DV

    def project(h, rope):
        b, n, _ = h.shape
        q, k, v, g = jnp.split(h @ w_in, [d_qk, 2 * d_qk, 2 * d_qk + d_v], axis=-1)
        heads = lambda t, d: t.reshape(b, n, RET_HEADS, d).transpose(0, 2, 1, 3).astype(jnp.float32)
        q = heads(q, RET_DK)
        k = heads(k, RET_DK) * (RET_DK ** -0.5)
        v = heads(v, RET_DV)
        if rope:
            q, k = axial_rope(q), axial_rope(k)
        return q, k, v, g

    def finish(o, g):
        mu = jnp.mean(o, axis=-1, keepdims=True)
        var = jnp.mean(jnp.square(o - mu), axis=-1, keepdims=True)
        o = (o - mu) * lax.rsqrt(var + LN_EPS)
        b, h, n, dv = o.shape
        o = o.transpose(0, 2, 1, 3).reshape(b, n, h * dv).astype(g.dtype)
        return (jax.nn.silu(g) * o) @ w_out

    qc, kc, vc, gc = project(h_ctx, False)
    zeros = jnp.zeros((h_ctx.shape[0], RET_HEADS, RET_DK, RET_DV), jnp.float32)
    o_ctx, s_f, s_b = retention_bidir(qc, kc, vc, zeros, zeros)
    ql, kl, vl, gl = project(h_lat, True)
    o_lat, _, _ = retention_bidir(ql, kl, vl, s_f, s_b)
    y_lat = finish(o_lat, gl)
    y_ctx = finish(o_ctx, gc) if with_ctx else None
    return y_lat, y_ctx


def neighborhood_mixer(h_lat, h_ctx, w_qkv, rpb, w_out, with_ctx):
    b, n, _ = h_lat.shape
    n_ctx = h_ctx.shape[1]
    rows = n // GRID_W
    kr = min(NA_WIN_R, rows)
    scale = NA_HEAD_DIM ** -0.5
    u = (h_lat @ w_qkv).reshape(b, rows, GRID_W, 3, NA_HEADS, NA_HEAD_DIM)
    q, k, v = u[:, :, :, 0], u[:, :, :, 1], u[:, :, :, 2]
    uc = (h_ctx @ w_qkv).reshape(b, n_ctx, 3, NA_HEADS, NA_HEAD_DIM)
    qc, kc, vc = uc[:, :, 0], uc[:, :, 1], uc[:, :, 2]

    col = jnp.arange(GRID_W)
    col_start = jnp.clip(col - NA_WIN_C // 2, 0, GRID_W - NA_WIN_C)
    col_ok = (col[None, :] >= col_start[:, None]) & (col[None, :] < col_start[:, None] + NA_WIN_C)
    rel_c = jnp.clip(col[None, :] - col[:, None] + NA_WIN_C - 1, 0, 2 * NA_WIN_C - 2)

    def row_block(r):
        rs = jnp.clip(r - kr // 2, 0, rows - kr)
        q_r = lax.dynamic_index_in_dim(q, r, axis=1, keepdims=False)
        k_r = lax.dynamic_slice_in_dim(k, rs, kr, axis=1)
        v_r = lax.dynamic_slice_in_dim(v, rs, kr, axis=1)
        rel_r = rs + jnp.arange(kr) - r + NA_WIN_R - 1
        bias = rpb[:, rel_r[None, :, None], rel_c[:, None, :]].astype(jnp.float32)
        bias = jnp.where(col_ok[None, :, None, :], bias, NEG_INF)
        s_loc = jnp.einsum('bqhd,bkwhd->bhqkw', q_r, k_r).astype(jnp.float32) * scale + bias
        s_loc = s_loc.reshape(b, NA_HEADS, GRID_W, kr * GRID_W)
        s_ctx = jnp.einsum('bqhd,blhd->bhql', q_r, kc).astype(jnp.float32) * scale
        p = jax.nn.softmax(jnp.concatenate([s_loc, s_ctx], axis=-1), axis=-1).astype(v.dtype)
        p_loc = p[..., :kr * GRID_W].reshape(b, NA_HEADS, GRID_W, kr, GRID_W)
        p_ctx = p[..., kr * GRID_W:]
        return jnp.einsum('bhqkw,bkwhd->bqhd', p_loc, v_r) + jnp.einsum('bhql,blhd->bqhd', p_ctx, vc)

    o = lax.map(row_block, jnp.arange(rows))
    y_lat = jnp.moveaxis(o, 0, 1).reshape(b, n, NA_HEADS * NA_HEAD_DIM) @ w_out
    y_ctx = None
    if with_ctx:
        s = jnp.einsum('bqhd,bkhd->bhqk', qc, kc).astype(jnp.float32) * scale
        p = jax.nn.softmax(s, axis=-1).astype(vc.dtype)
        oc = jnp.einsum('bhqk,bkhd->bqhd', p, vc).reshape(b, n_ctx, NA_HEADS * NA_HEAD_DIM)
        y_ctx = oc @ w_out
    return y_lat, y_ctx


def centred_depthwise_conv(x, w, bias):
    left = LRU_CONV_W // 2
    y = lax.conv_general_dilated(x, w[:, None, :], window_strides=(1,),
                                 padding=[(left, LRU_CONV_W - 1 - left)],
                                 dimension_numbers=('NWC', 'WIO', 'NWC'),
                                 feature_group_count=x.shape[-1])
    return y + bias


def linear_scan(a, u, h0):
    def combine(e1, e2):
        a1, b1 = e1
        a2, b2 = e2
        return a1 * a2, a2 * b1 + b2
    a_cum, h = lax.associative_scan(combine, (a, u), axis=1)
    return a_cum * h0[:, None, :] + h


def rglru_gates(x, w_a, b_a, w_x, b_x, lam):
    b, n, _ = x.shape
    xb = x.reshape(b, n, LRU_BLOCKS, LRU_BLOCK_W)
    r = jax.nn.sigmoid(jnp.einsum('bnki,kij->bnkj', xb, w_a.astype(jnp.float32)).reshape(b, n, LRU_WIDTH) + b_a.astype(jnp.float32))
    i = jax.nn.sigmoid(jnp.einsum('bnki,kij->bnkj', xb, w_x.astype(jnp.float32)).reshape(b, n, LRU_WIDTH) + b_x.astype(jnp.float32))
    log_a = -LRU_C * r * jax.nn.softplus(-lam.astype(jnp.float32))
    a = jnp.exp(log_a)
    return a, jnp.sqrt(-jnp.expm1(2.0 * log_a)) * (i * x)


def rglru_mixer(h_lat, h_ctx, w_in, conv_w, conv_b, w_a, b_a, w_x, b_x, lam, w_out, with_ctx):
    def branches(h):
        gate, xr = jnp.split(h @ w_in, 2, axis=-1)
        return gate, centred_depthwise_conv(xr, conv_w, conv_b).astype(jnp.float32)

    gate_ctx, x_ctx = branches(h_ctx)
    gate_lat, x_lat = branches(h_lat)
    h0 = jnp.zeros((h_ctx.shape[0], LRU_WIDTH), jnp.float32)
    outs_ctx, outs_lat = [], []
    for d in range(2):
        orient = (lambda t: jnp.flip(t, axis=1)) if d == 1 else (lambda t: t)
        a_c, u_c = rglru_gates(orient(x_ctx), w_a[d], b_a[d], w_x[d], b_x[d], lam[d])
        hc = linear_scan(a_c, u_c, h0)
        a_l, u_l = rglru_gates(orient(x_lat), w_a[d], b_a[d], w_x[d], b_x[d], lam[d])
        hl = linear_scan(a_l, u_l, hc[:, -1])
        outs_lat.append(orient(hl))
        if with_ctx:
            outs_ctx.append(orient(hc))
    y_lat = (jax.nn.gelu(gate_lat) * (outs_lat[0] + outs_lat[1]).astype(gate_lat.dtype)) @ w_out
    y_ctx = None
    if with_ctx:
        y_ctx = (jax.nn.gelu(gate_ctx) * (outs_ctx[0] + outs_ctx[1]).astype(gate_ctx.dtype)) @ w_out
    return y_lat, y_ctx


def setup_inputs(seed: int = 0) -> dict:
    key = jax.random.key(seed)
    ks = list(jax.random.split(key, 32))
    nrm = lambda shape, s: jax.random.normal(ks.pop(), shape, jnp.float32) * s
    D = D_MODEL
    x = nrm((BATCH, SEQ, D), 1.0)
    c = nrm((BATCH, D), 1.0)
    ctx = nrm((BATCH, CTX_LEN, D), 1.0)
    c_ctx = nrm((D,), 1.0)
    ada_w = nrm((DEPTH, D, N_MOD * D), D ** -0.5)
    ada_b = nrm((DEPTH, N_MOD * D), 0.02)
    ln_g = 1.0 + nrm((DEPTH, N_SUB, D), 0.02)
    ln_b = nrm((DEPTH, N_SUB, D), 0.02)
    ffn_w_in = nrm((DEPTH, 2, D, 2 * D_FF), D ** -0.5)
    ffn_w_out = nrm((DEPTH, 2, D_FF, D), D_FF ** -0.5 * DEEPNORM_BETA)
    ret_w_in = nrm((N_LAYERS_A, D, RET_HEADS * (2 * RET_DK + 2 * RET_DV)), D ** -0.5)
    ret_w_out = nrm((N_LAYERS_A, RET_HEADS * RET_DV, D), (RET_HEADS * RET_DV) ** -0.5 * DEEPNORM_BETA)
    na_w_qkv = nrm((N_LAYERS_B, D, 3 * NA_HEADS * NA_HEAD_DIM), D ** -0.5)
    na_rpb = nrm((N_LAYERS_B, NA_HEADS, 2 * NA_WIN_R - 1, 2 * NA_WIN_C - 1), 0.1)
    na_w_out = nrm((N_LAYERS_B, NA_HEADS * NA_HEAD_DIM, D), (NA_HEADS * NA_HEAD_DIM) ** -0.5 * DEEPNORM_BETA)
    lru_w_in = nrm((N_LAYERS_C, D, 2 * LRU_WIDTH), D ** -0.5)
    lru_conv_w = nrm((N_LAYERS_C, LRU_CONV_W, LRU_WIDTH), LRU_CONV_W ** -0.5)
    lru_conv_b = nrm((N_LAYERS_C, LRU_WIDTH), 0.02)
    lru_w_a = nrm((N_LAYERS_C, 2, LRU_BLOCKS, LRU_BLOCK_W, LRU_BLOCK_W), LRU_BLOCK_W ** -0.5)
    lru_b_a = nrm((N_LAYERS_C, 2, LRU_WIDTH), 0.02)
    lru_w_x = nrm((N_LAYERS_C, 2, LRU_BLOCKS, LRU_BLOCK_W, LRU_BLOCK_W), LRU_BLOCK_W ** -0.5)
    lru_b_x = nrm((N_LAYERS_C, 2, LRU_WIDTH), 0.02)
    u = jax.random.uniform(ks.pop(), (N_LAYERS_C, 2, LRU_WIDTH), jnp.float32, minval=0.9, maxval=0.999)
    a = u ** (1.0 / LRU_C)
    lru_lam = jnp.log(a) - jnp.log1p(-a)
    lru_w_out = nrm((N_LAYERS_C, LRU_WIDTH, D), LRU_WIDTH ** -0.5 * DEEPNORM_BETA)
    return {'x': x, 'c': c, 'ctx': ctx, 'c_ctx': c_ctx, 'ada_w': ada_w, 'ada_b': ada_b,
            'ln_g': ln_g, 'ln_b': ln_b, 'ffn_w_in': ffn_w_in, 'ffn_w_out': ffn_w_out,
            'ret_w_in': ret_w_in, 'ret_w_out': ret_w_out,
            'na_w_qkv': na_w_qkv, 'na_rpb': na_rpb, 'na_w_out': na_w_out,
            'lru_w_in': lru_w_in, 'lru_conv_w': lru_conv_w, 'lru_conv_b': lru_conv_b,
            'lru_w_a': lru_w_a, 'lru_b_a': lru_b_a, 'lru_w_x': lru_w_x, 'lru_b_x': lru_b_x,
            'lru_lam': lru_lam, 'lru_w_out': lru_w_out}


def reference(x, c, ctx, c_ctx, ada_w, ada_b, ln_g, ln_b, ffn_w_in, ffn_w_out,
              ret_w_in, ret_w_out, na_w_qkv, na_rpb, na_w_out,
              lru_w_in, lru_conv_w, lru_conv_b, lru_w_a, lru_b_a, lru_w_x, lru_b_x,
              lru_lam, lru_w_out):
    h_lat, h_ctx = x, ctx
    s_lat, s_ctx = jax.nn.silu(c), jax.nn.silu(c_ctx)
    for layer in range(DEPTH):
        last = layer == DEPTH - 1
        m_l = (s_lat @ ada_w[layer] + ada_b[layer]).reshape(-1, N_MOD, 1, D_MODEL)
        m_l = [m_l[:, j] for j in range(N_MOD)]
        m_c = (s_ctx @ ada_w[layer] + ada_b[layer]).reshape(N_MOD, D_MODEL)
        m_c = [m_c[j] for j in range(N_MOD)]
        g, bb = ln_g[layer], ln_b[layer]
        h_lat = half_ffn(h_lat, m_l[0], m_l[1], m_l[2], ffn_w_in[layer, 0], ffn_w_out[layer, 0], g[0], bb[0])
        h_ctx = half_ffn(h_ctx, m_c[0], m_c[1], m_c[2], ffn_w_in[layer, 0], ffn_w_out[layer, 0], g[0], bb[0])
        u_lat = modulate(h_lat, m_l[3], m_l[4])
        u_ctx = modulate(h_ctx, m_c[3], m_c[4])
        kind, idx = layer % N_MIXERS, layer // N_MIXERS
        if kind == 0:
            y_lat, y_ctx = retention_mixer(u_lat, u_ctx, ret_w_in[idx], ret_w_out[idx], not last)
        elif kind == 1:
            y_lat, y_ctx = neighborhood_mixer(u_lat, u_ctx, na_w_qkv[idx], na_rpb[idx], na_w_out[idx], not last)
        else:
            y_lat, y_ctx = rglru_mixer(u_lat, u_ctx, lru_w_in[idx], lru_conv_w[idx], lru_conv_b[idx],
                                       lru_w_a[idx], lru_b_a[idx], lru_w_x[idx], lru_b_x[idx],
                                       lru_lam[idx], lru_w_out[idx], not last)
        h_lat = post_norm(h_lat, m_l[5] * y_lat, g[1], bb[1])
        if not last:
            h_ctx = post_norm(h_ctx, m_c[5] * y_ctx, g[1], bb[1])
        h_lat = half_ffn(h_lat, m_l[6], m_l[7], m_l[8], ffn_w_in[layer, 1], ffn_w_out[layer, 1], g[2], bb[2])
        if not last:
            h_ctx = half_ffn(h_ctx, m_c[6], m_c[7], m_c[8], ffn_w_in[layer, 1], ffn_w_out[layer, 1], g[2], bb[2])
    return h_lat
```

```python
import functools

import jax
import jax.numpy as jnp
from jax import lax
from jax.experimental import pallas as pl
from jax.experimental.pallas import tpu as pltpu

F32 = jnp.float32
BF16 = jnp.bfloat16

GRID_W = 64
N_MOD = 9
FFN_RES = 0.5
RET_DK = 256
RET_DV = 512
RET_CHUNK = 128
ROPE_BASE = 10000.0
NA_HEAD_DIM = 64
NA_WIN_R = 8
NA_WIN_C = 16
NA_ROWS_PER_BLOCK = 4
NEG_INF = -1e30
LRU_BLOCK_W = 256
LRU_CONV_W = 4
LRU_C = 8.0
LN_EPS = 1e-5

TM = 256
LRU_CHUNK = 256
SUBLANES = 8
VMEM_LIMIT = 56 * 1024 * 1024


def _cparams(n_axes):
    return pltpu.CompilerParams(
        dimension_semantics=("arbitrary",) * n_axes, vmem_limit_bytes=VMEM_LIMIT)


def _resident(block_shape, index_map):
    return pl.BlockSpec(block_shape, index_map, pipeline_mode=pl.Buffered(1))


def _silu(x):
    return x * (1.0 / (1.0 + jnp.exp(-x)))


def _sigmoid(x):
    return 1.0 / (1.0 + jnp.exp(-x))


def _gelu_tanh(x):
    return 0.5 * x * (1.0 + jnp.tanh(0.7978845608028654 * (x + 0.044715 * (x * x * x))))


def _layer_norm(z, g, b):
    mu = jnp.mean(z, axis=-1, keepdims=True)
    zc = z - mu
    var = jnp.mean(zc * zc, axis=-1, keepdims=True)
    return zc * lax.rsqrt(var + LN_EPS) * g + b


def _dot(a, b):
    return jnp.dot(a, b, preferred_element_type=F32)


def _mods_kernel(s_ref, w_ref, b_ref, o_ref):
    s = _silu(s_ref[...]).astype(BF16)
    o_ref[0] = _dot(s, w_ref[0].astype(BF16)) + b_ref[0]


def _mods_call(s, ada_w, ada_b):
    depth, d, n = ada_w.shape
    tn = 1024
    rows = s.shape[0]
    return pl.pallas_call(
        _mods_kernel,
        grid=(depth, n // tn),
        in_specs=[
            pl.BlockSpec((rows, d), lambda l, j: (0, 0)),
            pl.BlockSpec((1, d, tn), lambda l, j: (l, 0, j)),
            pl.BlockSpec((1, 1, tn), lambda l, j: (l, 0, j)),
        ],
        out_specs=pl.BlockSpec((1, rows, tn), lambda l, j: (l, 0, j)),
        out_shape=jax.ShapeDtypeStruct((depth, rows, n), F32),
        compiler_params=_cparams(2),
        name="adaln_mods",
    )(s, ada_w, ada_b.reshape(depth, 1, n))


def _ffn_kernel(h_ref, m_ref, win_ref, wout_ref, g_ref, b_ref, o_ref, *, j0, alpha, d_ff):
    h = h_ref[0]
    m = m_ref[0, 0]
    u = (h * (1.0 + m[j0 + 1:j0 + 2]) + m[j0:j0 + 1]).astype(BF16)
    gu = _dot(u, win_ref[...])
    a = (_silu(gu[:, :d_ff]) * gu[:, d_ff:]).astype(BF16)
    y = _dot(a, wout_ref[...])
    z = alpha * h + m[j0 + 2:j0 + 3] * (FFN_RES * y)
    o_ref[0] = _layer_norm(z, g_ref[...], b_ref[...])


def _ffn_call(h, mods, w_in, w_out, g, b, *, j0, alpha, n_ctx_tiles, lat_only):
    bsz, t_all, d = h.shape
    d_ff = w_out.shape[0]
    n_tiles = t_all // TM
    off = n_ctx_tiles if lat_only else 0
    grid_t = n_tiles - off

    def mod_map(bi, ti):
        return (jnp.where(ti + off >= n_ctx_tiles, 1, 0), bi, 0, 0)

    return pl.pallas_call(
        functools.partial(_ffn_kernel, j0=j0, alpha=alpha, d_ff=d_ff),
        grid=(bsz, grid_t),
        in_specs=[
            pl.BlockSpec((1, TM, d), lambda bi, ti: (bi, ti + off, 0)),
            pl.BlockSpec((1, 1, N_MOD, d), mod_map),
            _resident((d, 2 * d_ff), lambda bi, ti: (0, 0)),
            _resident((d_ff, d), lambda bi, ti: (0, 0)),
            pl.BlockSpec((1, d), lambda bi, ti: (0, 0)),
            pl.BlockSpec((1, d), lambda bi, ti: (0, 0)),
        ],
        out_specs=pl.BlockSpec((1, TM, d), lambda bi, ti: (bi, ti, 0)),
        out_shape=jax.ShapeDtypeStruct((bsz, grid_t * TM, d), F32),
        compiler_params=_cparams(2),
        name="half_ffn",
    )(h, mods, w_in, w_out, g, b)


def _finish_kernel(h_ref, m_ref, *rest, alpha, mode):
    if mode == "product":
        a_ref, c_ref, w_ref, g_ref, b_ref, o_ref = rest
        act = a_ref[0] * c_ref[0]
    else:
        a_ref, w_ref, g_ref, b_ref, o_ref = rest
        act = a_ref[0, 0].T.astype(BF16)
    h = h_ref[0]
    m = m_ref[0, 0]
    y = _dot(act, w_ref[...])
    z = alpha * h + m[5:6] * y
    o_ref[0] = _layer_norm(z, g_ref[...], b_ref[...])


def _finish_call(h, mods, acts, w_out, g, b, *, alpha, n_ctx_tiles, mode):
    bsz, t_all, d = h.shape
    k = w_out.shape[0]
    n_tiles = t_all // TM

    def mod_map(bi, ti):
        return (jnp.where(ti >= n_ctx_tiles, 1, 0), bi, 0, 0)

    if mode == "product":
        act_specs = [pl.BlockSpec((1, TM, k), lambda bi, ti: (bi, ti, 0)) for _ in acts]
    else:
        act_specs = [pl.BlockSpec((1, 1, k, TM), lambda bi, ti: (bi, ti, 0, 0))]
    return pl.pallas_call(
        functools.partial(_finish_kernel, alpha=alpha, mode=mode),
        grid=(bsz, n_tiles),
        in_specs=[
            pl.BlockSpec((1, TM, d), lambda bi, ti: (bi, ti, 0)),
            pl.BlockSpec((1, 1, N_MOD, d), mod_map),
            *act_specs,
            _resident((k, d), lambda bi, ti: (0, 0)),
            pl.BlockSpec((1, d), lambda bi, ti: (0, 0)),
            pl.BlockSpec((1, d), lambda bi, ti: (0, 0)),
        ],
        out_specs=pl.BlockSpec((1, TM, d), lambda bi, ti: (bi, ti, 0)),
        out_shape=jax.ShapeDtypeStruct((bsz, t_all, d), F32),
        compiler_params=_cparams(2),
        name="mixer_out_" + mode,
    )(h, mods, *acts, w_out, g, b)


def _modulated(h_ref, m_ref):
    m = m_ref[0, 0]
    return (h_ref[0] * (1.0 + m[4:5]) + m[3:4]).astype(BF16)


def _proj_in_specs(d, n_ctx_tiles, w_shape):
    def mod_map(bi, ti):
        return (jnp.where(ti >= n_ctx_tiles, 1, 0), bi, 0, 0)

    return [
        pl.BlockSpec((1, TM, d), lambda bi, ti: (bi, ti, 0)),
        pl.BlockSpec((1, 1, N_MOD, d), mod_map),
        _resident(w_shape, lambda bi, ti: (0, 0)),
    ]


def _ret_proj_kernel(h_ref, m_ref, w_ref, cos_ref, sin_ref, q_ref, k_ref, v_ref, g_ref, *, heads):
    p = _dot(_modulated(h_ref, m_ref), w_ref[...])
    d_qk = heads * RET_DK
    d_v = heads * RET_DV
    cos = cos_ref[...]
    sin = sin_ref[...]

    def rope(x):
        parts = []
        for j in range(RET_DK // 128):
            sl = slice(j * 128, (j + 1) * 128)
            xs = x[:, sl]
            parts.append(xs * cos[:, sl] + pltpu.roll(xs, 64, 1) * sin[:, sl])
        return jnp.concatenate(parts, axis=1)

    for hh in range(heads):
        sl = slice(hh * RET_DK, (hh + 1) * RET_DK)
        q_ref[0, :, sl] = rope(p[:, sl]).astype(BF16)
        k_ref[0, :, sl] = rope(p[:, d_qk + hh * RET_DK:d_qk + (hh + 1) * RET_DK]
                               * (RET_DK ** -0.5)).astype(BF16)
    v_ref[0] = p[:, 2 * d_qk:2 * d_qk + d_v].astype(BF16)
    g_ref[0] = _silu(p[:, 2 * d_qk + d_v:]).astype(BF16)


def _ret_proj_call(h, mods, w_in, cos, sin, *, n_ctx_tiles):
    bsz, t_all, d = h.shape
    n = w_in.shape[1]
    heads = n // (2 * RET_DK + 2 * RET_DV)
    d_qk, d_v = heads * RET_DK, heads * RET_DV
    tile = lambda w: pl.BlockSpec((1, TM, w), lambda bi, ti: (bi, ti, 0))
    shape = lambda w: jax.ShapeDtypeStruct((bsz, t_all, w), BF16)
    return pl.pallas_call(
        functools.partial(_ret_proj_kernel, heads=heads),
        grid=(bsz, t_all // TM),
        in_specs=_proj_in_specs(d, n_ctx_tiles, (d, n)) + [
            pl.BlockSpec((TM, RET_DK), lambda bi, ti: (ti, 0)),
            pl.BlockSpec((TM, RET_DK), lambda bi, ti: (ti, 0)),
        ],
        out_specs=[tile(d_qk), tile(d_qk), tile(d_v), tile(d_v)],
        out_shape=[shape(d_qk), shape(d_qk), shape(d_v), shape(d_v)],
        compiler_params=_cparams(2),
        name="retention_proj",
    )(h, mods, w_in, cos, sin)


def _ret_core_kernel(q_ref, k_ref, v_ref, mask_ref, qdf_ref, qdb_ref, kdf_ref, kdb_ref,
                     cdf_ref, cdb_ref, o_ref, of_s, ob_s, sf_s, sb_s, *, n_chunks, n_ctx_chunks):
    c = RET_CHUNK
    sf_s[...] = jnp.zeros_like(sf_s)
    sb_s[...] = jnp.zeros_like(sb_s)
    nt = (((1,), (1,)), ((), ()))
    tn = (((0,), (0,)), ((), ()))

    def step(i, carry):
        rf = pl.multiple_of(i * c, c)
        q = q_ref[0, pl.ds(rf, c), :]
        k = k_ref[0, pl.ds(rf, c), :]
        v = v_ref[0, pl.ds(rf, c), :]
        s = lax.dot_general(q, k, nt, preferred_element_type=F32)
        att = (s * mask_ref[0]).astype(BF16)
        of_s[pl.ds(rf, c), :] = _dot(att, v) + qdf_ref[0] * _dot(q, sf_s[...].astype(BF16))
        kd = (k * kdf_ref[0]).astype(BF16)
        sf_s[...] = sf_s[...] * cdf_ref[0] + lax.dot_general(kd, v, tn, preferred_element_type=F32)
        cb = jnp.where(i < n_ctx_chunks, n_ctx_chunks - 1 - i, n_chunks - 1 - (i - n_ctx_chunks))
        rb = pl.multiple_of(cb * c, c)
        q = q_ref[0, pl.ds(rb, c), :]
        k = k_ref[0, pl.ds(rb, c), :]
        v = v_ref[0, pl.ds(rb, c), :]
        ob_s[pl.ds(rb, c), :] = qdb_ref[0] * _dot(q, sb_s[...].astype(BF16))
        kd = (k * kdb_ref[0]).astype(BF16)
        sb_s[...] = sb_s[...] * cdb_ref[0] + lax.dot_general(kd, v, tn, preferred_element_type=F32)
        return carry

    lax.fori_loop(0, n_chunks, step, 0)

    def norm(i, carry):
        r = pl.multiple_of(i * c, c)
        o = of_s[pl.ds(r, c), :] + ob_s[pl.ds(r, c), :]
        mu = jnp.mean(o, axis=-1, keepdims=True)
        oc = o - mu
        var = jnp.mean(oc * oc, axis=-1, keepdims=True)
        o_ref[0, pl.ds(r, c), :] = (oc * lax.rsqrt(var + LN_EPS)).astype(BF16)
        return carry

    lax.fori_loop(0, n_chunks, norm, 0)


def _ret_tables(heads):
    c = RET_CHUNK
    hs = jnp.arange(heads, dtype=F32)
    lgf = jnp.log1p(-jnp.exp2(-5.0 - hs))
    lgb = lgf[::-1]
    pos = jnp.arange(c, dtype=F32)
    diff = pos[:, None] - pos[None, :]
    mask = jnp.where(diff >= 0,
                     jnp.exp(lgf[:, None, None] * jnp.maximum(diff, 0.0)),
                     jnp.exp(lgb[:, None, None] * jnp.maximum(-diff, 0.0)))
    wide = lambda t, w: jnp.broadcast_to(t[:, :, None], (heads, c, w))
    qdf = wide(jnp.exp(lgf[:, None] * (pos + 1.0)), RET_DV)
    qdb = wide(jnp.exp(lgb[:, None] * (c - pos)), RET_DV)
    kdf = wide(jnp.exp(lgf[:, None] * (c - 1.0 - pos)), RET_DK)
    kdb = wide(jnp.exp(lgb[:, None] * pos), RET_DK)
    cdf = jnp.broadcast_to(jnp.exp(lgf * c)[:, None, None], (heads, 1, RET_DV))
    cdb = jnp.broadcast_to(jnp.exp(lgb * c)[:, None, None], (heads, 1, RET_DV))
    return mask, qdf, qdb, kdf, kdb, cdf, cdb


def _ret_core_call(q, k, v, *, ctx_len):
    bsz, t_all, d_qk = q.shape
    heads = d_qk // RET_DK
    c = RET_CHUNK
    tables = _ret_tables(heads)
    per_head = lambda a: pl.BlockSpec((1,) + a.shape[1:], lambda bi, hi: (hi, 0, 0))
    return pl.pallas_call(
        functools.partial(_ret_core_kernel, n_chunks=t_all // c, n_ctx_chunks=ctx_len // c),
        grid=(bsz, heads),
        in_specs=[
            pl.BlockSpec((1, t_all, RET_DK), lambda bi, hi: (bi, 0, hi)),
            pl.BlockSpec((1, t_all, RET_DK), lambda bi, hi: (bi, 0, hi)),
            pl.BlockSpec((1, t_all, RET_DV), lambda bi, hi: (bi, 0, hi)),
            *[per_head(a) for a in tables],
        ],
        out_specs=pl.BlockSpec((1, t_all, RET_DV), lambda bi, hi: (bi, 0, hi)),
        out_shape=jax.ShapeDtypeStruct((bsz, t_all, heads * RET_DV), BF16),
        scratch_shapes=[
            pltpu.VMEM((t_all, RET_DV), F32),
            pltpu.VMEM((t_all, RET_DV), F32),
            pltpu.VMEM((RET_DK, RET_DV), F32),
            pltpu.VMEM((RET_DK, RET_DV), F32),
        ],
        compiler_params=_cparams(2),
        name="retention_core",
    )(q, k, v, *tables)


def _rope_tables(seq, ctx_len):
    half = RET_DK // 2
    freqs = ROPE_BASE ** (-jnp.arange(0, half, 2, dtype=F32) / half)
    pos = jnp.arange(seq)
    ar = (pos // GRID_W).astype(F32)[:, None] * freqs
    ac = (pos % GRID_W).astype(F32)[:, None] * freqs
    cos = jnp.concatenate([jnp.cos(ar), jnp.cos(ar), jnp.cos(ac), jnp.cos(ac)], axis=-1)
    sin = jnp.concatenate([-jnp.sin(ar), jnp.sin(ar), -jnp.sin(ac), jnp.sin(ac)], axis=-1)
    cos = jnp.concatenate([jnp.ones((ctx_len, RET_DK), F32), cos], axis=0)
    sin = jnp.concatenate([jnp.zeros((ctx_len, RET_DK), F32), sin], axis=0)
    return cos, sin


def _na_proj_kernel(h_ref, m_ref, w_ref, k_ref, qt_ref, vt_ref, *, d_att):
    p = _dot(_modulated(h_ref, m_ref), w_ref[...])
    qt_ref[0, 0] = (p[:, :d_att] * (NA_HEAD_DIM ** -0.5)).T.astype(BF16)
    k_ref[0] = p[:, d_att:2 * d_att].astype(BF16)
    vt_ref[0, 0] = p[:, 2 * d_att:].T.astype(BF16)


def _na_proj_call(h, mods, w_qkv, *, n_ctx_tiles):
    bsz, t_all, d = h.shape
    d_att = w_qkv.shape[1] // 3
    n_tiles = t_all // TM
    t_spec = pl.BlockSpec((1, 1, d_att, TM), lambda bi, ti: (bi, ti, 0, 0))
    t_shape = jax.ShapeDtypeStruct((bsz, n_tiles, d_att, TM), BF16)
    return pl.pallas_call(
        functools.partial(_na_proj_kernel, d_att=d_att),
        grid=(bsz, n_tiles),
        in_specs=_proj_in_specs(d, n_ctx_tiles, (d, 3 * d_att)),
        out_specs=[pl.BlockSpec((1, TM, d_att), lambda bi, ti: (bi, ti, 0)), t_spec, t_spec],
        out_shape=[jax.ShapeDtypeStruct((bsz, t_all, d_att), BF16), t_shape, t_shape],
        compiler_params=_cparams(2),
        name="na_proj",
    )(h, mods, w_qkv)


def _na_attn_kernel(k_ref, qt_ref, vt_ref, bi_ref, bf_ref, bl_ref, ot_ref, *, ctx_len, n_blocks):
    hd = NA_HEAD_DIM
    qb = NA_ROWS_PER_BLOCK * GRID_W
    n_ctx_tiles = ctx_len // TM
    row = lax.broadcasted_iota(jnp.int32, (2 * hd, qb), 0)
    k_ctx = k_ref[0, 0:ctx_len, :]

    def attend(q_tile, k_off, n_keys, bias_ref):
        qt2 = qt_ref[0, q_tile]
        outs = []
        for hh in range(2):
            qt = jnp.where((row >= hd) if hh else (row < hd), qt2, jnp.zeros_like(qt2))
            s_ctx = _dot(k_ctx, qt)
            m = jnp.max(s_ctx, axis=0, keepdims=True)
            if n_keys:
                s_loc = _dot(k_ref[0, pl.ds(k_off, n_keys), :], qt) + bias_ref[hh]
                m = jnp.maximum(m, jnp.max(s_loc, axis=0, keepdims=True))
            p_ctx = jnp.exp(s_ctx - m)
            l = jnp.sum(p_ctx, axis=0, keepdims=True)
            o = jnp.zeros((hd, qb), F32)
            for j in range(n_ctx_tiles):
                o = o + _dot(vt_ref[0, j, hh * hd:(hh + 1) * hd, :],
                             p_ctx[j * TM:(j + 1) * TM].astype(BF16))
            if n_keys:
                p_loc = jnp.exp(s_loc - m)
                l = l + jnp.sum(p_loc, axis=0, keepdims=True)
                k_tile = k_off // TM
                for j in range(-(-n_keys // TM)):
                    w = min(TM, n_keys - j * TM)
                    o = o + _dot(vt_ref[0, k_tile + j, hh * hd:(hh + 1) * hd, 0:w],
                                 p_loc[j * TM:j * TM + w].astype(BF16))
            outs.append(o * (1.0 / l))
        ot_ref[0, q_tile] = jnp.concatenate(outs, axis=0)

    for j in range(n_ctx_tiles):
        attend(j, 0, 0, None)
    edge_keys = NA_WIN_R * GRID_W
    attend(n_ctx_tiles, ctx_len, edge_keys, bf_ref)
    last = n_blocks - 1
    attend(n_ctx_tiles + last, ctx_len + (last + 1) * qb - edge_keys, edge_keys, bl_ref)
    inner_keys = (NA_WIN_R + NA_ROWS_PER_BLOCK - 1) * GRID_W

    def inner(i, carry):
        k_off = pl.multiple_of(ctx_len + (i - 1) * qb, qb)
        attend(n_ctx_tiles + i, k_off, inner_keys, bi_ref)
        return carry

    lax.fori_loop(1, n_blocks - 1, inner, 0)


def _na_bias_tables(rpb):
    heads = rpb.shape[0]
    w = GRID_W
    col = jnp.arange(w)
    start = jnp.clip(col - NA_WIN_C // 2, 0, w - NA_WIN_C)
    ok = (col[:, None] >= start[None, :]) & (col[:, None] < start[None, :] + NA_WIN_C)
    rel_c = jnp.clip(col[:, None] - col[None, :] + NA_WIN_C - 1, 0, 2 * NA_WIN_C - 2)
    colb = jnp.where(ok[None, None], rpb[:, :, rel_c].astype(F32), NEG_INF)
    masked = jnp.full((heads, w, w), NEG_INF, F32)

    def build(n_key_rows, rel_fn):
        rows = []
        for kj in range(n_key_rows):
            blocks = []
            for qi in range(NA_ROWS_PER_BLOCK):
                rr = rel_fn(kj, qi)
                blocks.append(masked if rr is None else colb[:, rr])
            rows.append(jnp.concatenate(blocks, axis=2))
        return jnp.concatenate(rows, axis=1)

    inner = build(NA_WIN_R + NA_ROWS_PER_BLOCK - 1,
                  lambda kj, qi: kj - qi + NA_WIN_R // 2 - 1 if 0 <= kj - qi < NA_WIN_R else None)
    first = build(NA_WIN_R, lambda kj, qi: kj - qi + NA_WIN_R - 1)
    last = build(NA_WIN_R, lambda kj, qi: kj - qi + NA_WIN_R // 2 - 1)
    return inner, first, last


def _na_attn_call(k, qt, vt, rpb, *, ctx_len):
    bsz, t_all, d_att = k.shape
    n_tiles = t_all // TM
    pairs = d_att // (2 * NA_HEAD_DIM)
    n_blocks = (t_all - ctx_len) // (NA_ROWS_PER_BLOCK * GRID_W)
    b_inner, b_first, b_last = _na_bias_tables(rpb)
    t_spec = pl.BlockSpec((1, n_tiles, 2 * NA_HEAD_DIM, TM), lambda bi, pi: (bi, 0, pi, 0))
    bias_spec = lambda a: pl.BlockSpec((2,) + a.shape[1:], lambda bi, pi: (pi, 0, 0))
    return pl.pallas_call(
        functools.partial(_na_attn_kernel, ctx_len=ctx_len, n_blocks=n_blocks),
        grid=(bsz, pairs),
        in_specs=[
            pl.BlockSpec((1, t_all, 2 * NA_HEAD_DIM), lambda bi, pi: (bi, 0, pi)),
            t_spec, t_spec, bias_spec(b_inner), bias_spec(b_first), bias_spec(b_last),
        ],
        out_specs=t_spec,
        out_shape=jax.ShapeDtypeStruct((bsz, n_tiles, d_att, TM), F32),
        compiler_params=_cparams(2),
        name="na_attention",
    )(k, qt, vt, b_inner, b_first, b_last)


def _lru_proj_kernel(h_ref, m_ref, w_ref, gate_ref, x_ref, *, width):
    p = _dot(_modulated(h_ref, m_ref), w_ref[...])
    gate_ref[0] = _gelu_tanh(p[:, :width]).astype(BF16)
    x_ref[0] = p[:, width:]


def _lru_proj_call(h, mods, w_in, *, n_ctx_tiles):
    bsz, t_all, d = h.shape
    width = w_in.shape[1] // 2
    tile = pl.BlockSpec((1, TM, width), lambda bi, ti: (bi, ti, 0))
    return pl.pallas_call(
        functools.partial(_lru_proj_kernel, width=width),
        grid=(bsz, t_all // TM),
        in_specs=_proj_in_specs(d, n_ctx_tiles, (d, 2 * width)),
        out_specs=[tile, tile],
        out_shape=[jax.ShapeDtypeStruct((bsz, t_all, width), BF16),
                   jax.ShapeDtypeStruct((bsz, t_all, width), F32)],
        compiler_params=_cparams(2),
        name="rglru_proj",
    )(h, mods, w_in)


def _lru_core_kernel(x_ref, cw_ref, cb_ref, wa_ref, ba_ref, wx_ref, bx_ref, lam_ref, o_ref,
                     xp_s, a_s, u_s, *, t_all, ctx_len):
    pad = SUBLANES
    ch = LRU_CHUNK
    bw = LRU_BLOCK_W
    n_chunks = t_all // ch
    n_ctx_chunks = ctx_len // ch

    xp_s[0:pad, :] = jnp.zeros((pad, bw), F32)
    xp_s[pad + t_all:2 * pad + t_all, :] = jnp.zeros((pad, bw), F32)

    def copy(i, carry):
        r = pl.multiple_of(i * ch, ch)
        xp_s[pl.ds(r + pad, ch), :] = x_ref[0, pl.ds(r, ch), :]
        return carry

    lax.fori_loop(0, n_chunks, copy, 0)

    cw = cw_ref[...]
    cb = cb_ref[...]
    lam = lam_ref[...]
    neg_lam = -lam
    softplus = jnp.maximum(neg_lam, 0.0) + jnp.log1p(jnp.exp(-jnp.abs(neg_lam)))
    decay_rate = -LRU_C * softplus
    rowi = lax.broadcasted_iota(jnp.int32, (ch, bw), 0)

    def gates(i, carry):
        r = pl.multiple_of(i * ch, ch)
        xe = xp_s[pl.ds(r, ch + 2 * pad), :]
        mid = slice(pad, pad + ch)
        x_m2 = pltpu.roll(xe, 2, 0)[mid]
        x_m1 = pltpu.roll(xe, 1, 0)[mid]
        x_p1 = pltpu.roll(xe, ch + 2 * pad - 1, 0)[mid]
        first_lat = i == n_ctx_chunks
        last_ctx = i == n_ctx_chunks - 1
        x_m2 = jnp.where(rowi < jnp.where(first_lat, 2, 0), 0.0, x_m2)
        x_m1 = jnp.where(rowi < jnp.where(first_lat, 1, 0), 0.0, x_m1)
        x_p1 = jnp.where(rowi >= jnp.where(last_ctx, ch - 1, ch), 0.0, x_p1)
        x = cw[0:1] * x_m2 + cw[1:2] * x_m1 + cw[2:3] * xe[mid] + cw[3:4] * x_p1 + cb
        xb = x.astype(BF16)
        for d in range(2):
            rg = _sigmoid(_dot(xb, wa_ref[d, 0]) + ba_ref[d:d + 1])
            ig = _sigmoid(_dot(xb, wx_ref[d, 0]) + bx_ref[d:d + 1])
            log_a = rg * decay_rate[d:d + 1]
            a_s[d, pl.ds(r, ch), :] = jnp.exp(log_a)
            th = jnp.tanh(log_a)
            u_s[d, pl.ds(r, ch), :] = jnp.sqrt(-2.0 * th / (1.0 - th)) * (ig * x)
        return carry

    lax.fori_loop(0, n_chunks, gates, 0)

    sub = lax.broadcasted_iota(jnp.int32, (SUBLANES, bw), 0)
    n_groups = t_all // SUBLANES
    n_ctx_groups = ctx_len // SUBLANES

    def scan8(a, u, reverse):
        for s in (1, 2, 4):
            if reverse:
                valid = sub < SUBLANES - s
                shift = SUBLANES - s
            else:
                valid = sub >= s
                shift = s
            u = u + a * jnp.where(valid, pltpu.roll(u, shift, 0), 0.0)
            a = a * jnp.where(valid, pltpu.roll(a, shift, 0), 1.0)
        return a, u

    def rec(i, carry):
        hf, hb = carry
        r = pl.multiple_of(i * SUBLANES, SUBLANES)
        a, u = scan8(a_s[0, pl.ds(r, SUBLANES), :], u_s[0, pl.ds(r, SUBLANES), :], False)
        h = u + a * hf
        u_s[0, pl.ds(r, SUBLANES), :] = h
        hf = h[SUBLANES - 1:SUBLANES]
        gb = jnp.where(i < n_ctx_groups, n_ctx_groups - 1 - i, n_groups - 1 - (i - n_ctx_groups))
        r = pl.multiple_of(gb * SUBLANES, SUBLANES)
        a, u = scan8(a_s[1, pl.ds(r, SUBLANES), :], u_s[1, pl.ds(r, SUBLANES), :], True)
        h = u + a * hb
        u_s[1, pl.ds(r, SUBLANES), :] = h
        hb = h[0:1]
        return hf, hb

    zero = jnp.zeros((1, bw), F32)
    lax.fori_loop(0, n_groups, rec, (zero, zero))

    def emit(i, carry):
        r = pl.multiple_of(i * ch, ch)
        o_ref[0, pl.ds(r, ch), :] = (u_s[0, pl.ds(r, ch), :] + u_s[1, pl.ds(r, ch), :]).astype(BF16)
        return carry

    lax.fori_loop(0, n_chunks, emit, 0)


def _lru_core_call(x, conv_w, conv_b, w_a, b_a, w_x, b_x, lam, *, ctx_len):
    bsz, t_all, width = x.shape
    bw = LRU_BLOCK_W
    blocks = width // bw
    col = lambda rows: pl.BlockSpec((rows, bw), lambda bi, ki: (0, ki))
    wspec = pl.BlockSpec((2, 1, bw, bw), lambda bi, ki: (0, ki, 0, 0))
    return pl.pallas_call(
        functools.partial(_lru_core_kernel, t_all=t_all, ctx_len=ctx_len),
        grid=(bsz, blocks),
        in_specs=[
            pl.BlockSpec((1, t_all, bw), lambda bi, ki: (bi, 0, ki)),
            col(LRU_CONV_W), col(1), wspec, col(2), wspec, col(2), col(2),
        ],
        out_specs=pl.BlockSpec((1, t_all, bw), lambda bi, ki: (bi, 0, ki)),
        out_shape=jax.ShapeDtypeStruct((bsz, t_all, width), BF16),
        scratch_shapes=[
            pltpu.VMEM((t_all + 2 * SUBLANES, bw), F32),
            pltpu.VMEM((2, t_all, bw), F32),
            pltpu.VMEM((2, t_all, bw), F32),
        ],
        compiler_params=_cparams(2),
        name="rglru_core",
    )(x, conv_w, conv_b, w_a, b_a, w_x, b_x, lam)


def kernel(x, c, ctx, c_ctx, ada_w, ada_b, ln_g, ln_b, ffn_w_in, ffn_w_out, ret_w_in, ret_w_out,
           na_w_qkv, na_rpb, na_w_out, lru_w_in, lru_conv_w, lru_conv_b, lru_w_a, lru_b_a,
           lru_w_x, lru_b_x, lru_lam, lru_w_out):
    bsz, seq, d = x.shape
    ctx_len = ctx.shape[1]
    depth = ada_w.shape[0]
    alpha = (2 * depth) ** 0.25
    assert ctx_len % TM == 0 and seq % TM == 0 and ctx_len % LRU_CHUNK == 0
    assert NA_ROWS_PER_BLOCK * GRID_W == TM and seq // TM >= 3
    n_ctx_tiles = ctx_len // TM

    s = jnp.concatenate([jnp.broadcast_to(c_ctx[None, :], (bsz, d)), c], axis=0)
    mods_all = _mods_call(s, ada_w, ada_b).reshape(depth, 2, bsz, N_MOD, d)
    h = jnp.concatenate([ctx, x], axis=1)
    cos, sin = _rope_tables(seq, ctx_len)
    bf = lambda w: w.astype(BF16)
    row = lambda v: v.reshape(1, d)

    for layer in range(depth):
        last = layer == depth - 1
        mods = mods_all[layer]
        g, b = ln_g[layer], ln_b[layer]
        common = dict(alpha=alpha, n_ctx_tiles=n_ctx_tiles)
        h = _ffn_call(h, mods, bf(ffn_w_in[layer, 0]), bf(ffn_w_out[layer, 0]), row(g[0]), row(b[0]),
                      j0=0, lat_only=False, **common)
        kind, idx = layer % 3, layer // 3
        if kind == 0:
            q, k, v, sg = _ret_proj_call(h, mods, bf(ret_w_in[idx]), cos, sin, n_ctx_tiles=n_ctx_tiles)
            o = _ret_core_call(q, k, v, ctx_len=ctx_len)
            h = _finish_call(h, mods, [sg, o], bf(ret_w_out[idx]), row(g[1]), row(b[1]),
                             mode="product", **common)
        elif kind == 1:
            k, qt, vt = _na_proj_call(h, mods, bf(na_w_qkv[idx]), n_ctx_tiles=n_ctx_tiles)
            ot = _na_attn_call(k, qt, vt, na_rpb[idx], ctx_len=ctx_len)
            h = _finish_call(h, mods, [ot], bf(na_w_out[idx]), row(g[1]), row(b[1]),
                             mode="transposed", **common)
        else:
            gate, xr = _lru_proj_call(h, mods, bf(lru_w_in[idx]), n_ctx_tiles=n_ctx_tiles)
            hs = _lru_core_call(xr, lru_conv_w[idx], lru_conv_b[idx].reshape(1, -1),
                                bf(lru_w_a[idx]), lru_b_a[idx], bf(lru_w_x[idx]), lru_b_x[idx],
                                lru_lam[idx], ctx_len=ctx_len)
            h = _finish_call(h, mods, [gate, hs], bf(lru_w_out[idx]), row(g[1]), row(b[1]),
                             mode="product", **common)
        h = _ffn_call(h, mods, bf(ffn_w_in[layer, 1]), bf(ffn_w_out[layer, 1]), row(g[2]), row(b[2]),
                      j0=6, lat_only=last, **common)
    return h
```

```python
import functools

import jax
import jax.numpy as jnp
from jax import lax
from jax.experimental import pallas as pl
from jax.experimental.pallas import tpu as pltpu

F32 = jnp.float32
BF16 = jnp.bfloat16

GRID_W = 64
N_MOD = 9
FFN_RES = 0.5
RET_DK = 256
RET_DV = 512
RET_BLOCK = 256
RET_UNROLL = 2
ROPE_BASE = 10000.0
NA_HEAD_DIM = 64
NA_WIN_R = 8
NA_WIN_C = 16
NA_ROWS_PER_BLOCK = 4
NA_WIN_TILES = 3
NA_BLOCKS_PER_ITER = 2
NEG_INF = -1e30
LOG2E = 1.4426950408889634
LRU_BLOCK_W = 256
LRU_CONV_W = 4
LRU_C = 8.0
LN_EPS = 1e-5

TM = 256
LRU_CHUNK = 256
SUBLANES = 8
BF16_SUBLANES = 16
SCAN_GROUPS = 8
VMEM_LIMIT = 56 * 1024 * 1024


def _cparams(n_axes):
    return pltpu.CompilerParams(
        dimension_semantics=("arbitrary",) * n_axes, vmem_limit_bytes=VMEM_LIMIT)


def _weight_spec(tail, *lead):
    zeros = (0,) * len(tail)
    return pl.BlockSpec((None,) * len(lead) + tuple(tail), lambda bi, ti: tuple(lead) + zeros,
                        pipeline_mode=pl.Buffered(1))


def _mod_spec(d, layer, n_ctx_tiles, off=0):
    return pl.BlockSpec(
        (None, None, None, N_MOD, d),
        lambda bi, ti: (layer, jnp.where(ti + off >= n_ctx_tiles, 1, 0), bi, 0, 0))


def _ln_spec(d, layer, j):
    return pl.BlockSpec((None, None, 1, d), lambda bi, ti: (layer, j, 0, 0))


def _tile_spec(width, off=0):
    return pl.BlockSpec((None, TM, width), lambda bi, ti: (bi, ti + off, 0))


def _silu(x):
    return x * (1.0 / (1.0 + jnp.exp(-x)))


def _sigmoid(x):
    return 0.5 * jnp.tanh(0.5 * x) + 0.5


def _gelu_tanh(x):
    return 0.5 * x * (1.0 + jnp.tanh(0.7978845608028654 * (x + 0.044715 * (x * x * x))))


def _layer_norm(z, g, b):
    mu = jnp.mean(z, axis=-1, keepdims=True)
    zc = z - mu
    var = jnp.mean(zc * zc, axis=-1, keepdims=True)
    return zc * lax.rsqrt(var + LN_EPS) * g + b


def _dot(a, b):
    return jnp.dot(a, b, preferred_element_type=F32)


def _half_ffn(h, m, j0, win_ref, wout_ref, g, b, alpha):
    d_ff = wout_ref.shape[0]
    u = (h * (1.0 + m[j0 + 1:j0 + 2]) + m[j0:j0 + 1]).astype(BF16)
    gu = _dot(u, win_ref[...])
    a = (_silu(gu[:, :d_ff]) * gu[:, d_ff:]).astype(BF16)
    y = _dot(a, wout_ref[...])
    return _layer_norm(alpha * h + m[j0 + 2:j0 + 3] * (FFN_RES * y), g, b)


def _modulated(h_ref, m_ref):
    m = m_ref[...]
    return (h_ref[...] * (1.0 + m[4:5]) + m[3:4]).astype(BF16)


def _mods_kernel(s_ref, w_ref, b_ref, o_ref):
    s = _silu(s_ref[...]).astype(BF16)
    o_ref[0] = _dot(s, w_ref[0].astype(BF16)) + b_ref[0]


def _mods_call(s, ada_w, ada_b):
    depth, d, n = ada_w.shape
    tn = 1024
    rows = s.shape[0]
    return pl.pallas_call(
        _mods_kernel,
        grid=(depth, n // tn),
        in_specs=[
            pl.BlockSpec((rows, d), lambda l, j: (0, 0)),
            pl.BlockSpec((1, d, tn), lambda l, j: (l, 0, j)),
            pl.BlockSpec((1, 1, tn), lambda l, j: (l, 0, j)),
        ],
        out_specs=pl.BlockSpec((1, rows, tn), lambda l, j: (l, 0, j)),
        out_shape=jax.ShapeDtypeStruct((depth, rows, n), F32),
        compiler_params=_cparams(2),
        name="adaln_mods",
    )(s, ada_w, ada_b.reshape(depth, 1, n))


def _ffn_in_kernel(*refs, alpha, n_ctx_tiles, split):
    if split:
        c_ref, x_ref, m_ref, win_ref, wout_ref, g_ref, b_ref, o_ref = refs
        h = jnp.where(pl.program_id(1) < n_ctx_tiles, c_ref[...], x_ref[...])
    else:
        h_ref, m_ref, win_ref, wout_ref, g_ref, b_ref, o_ref = refs
        h = h_ref[...]
    o_ref[...] = _half_ffn(h, m_ref[...], 0, win_ref, wout_ref, g_ref[...], b_ref[...], alpha)


def _ffn_in_call(srcs, mods, w_in, w_out, ln_g, ln_b, *, layer, alpha, n_ctx_tiles, t_all):
    bsz, _, d = srcs[0].shape
    d_ff = w_out.shape[-2]
    split = len(srcs) == 2
    if split:
        src_specs = [
            pl.BlockSpec((None, TM, d), lambda bi, ti: (bi, jnp.minimum(ti, n_ctx_tiles - 1), 0)),
            pl.BlockSpec((None, TM, d), lambda bi, ti: (bi, jnp.maximum(ti - n_ctx_tiles, 0), 0)),
        ]
    else:
        src_specs = [_tile_spec(d)]
    return pl.pallas_call(
        functools.partial(_ffn_in_kernel, alpha=alpha, n_ctx_tiles=n_ctx_tiles, split=split),
        grid=(bsz, t_all // TM),
        in_specs=src_specs + [
            _mod_spec(d, layer, n_ctx_tiles),
            _weight_spec((d, 2 * d_ff), layer, 0),
            _weight_spec((d_ff, d), layer, 0),
            _ln_spec(d, layer, 0), _ln_spec(d, layer, 0),
        ],
        out_specs=_tile_spec(d),
        out_shape=jax.ShapeDtypeStruct((bsz, t_all, d), F32),
        compiler_params=_cparams(2),
        name="ffn_in",
    )(*srcs, mods, w_in, w_out, ln_g, ln_b)


def _post_kernel(h_ref, m_ref, *rest, alpha, mode):
    if mode == "product":
        a_ref, c_ref, wmix_ref, g1_ref, b1_ref, win_ref, wout_ref, g2_ref, b2_ref, o_ref = rest
        act = a_ref[...] * c_ref[...]
    else:
        a_ref, wmix_ref, g1_ref, b1_ref, win_ref, wout_ref, g2_ref, b2_ref, o_ref = rest
        act = a_ref[...].T.astype(BF16)
    m = m_ref[...]
    h = _layer_norm(alpha * h_ref[...] + m[5:6] * _dot(act, wmix_ref[...]), g1_ref[...], b1_ref[...])
    o_ref[...] = _half_ffn(h, m, 6, win_ref, wout_ref, g2_ref[...], b2_ref[...], alpha)


def _post_call(h, mods, acts, w_mix, w_in, w_out, ln_g, ln_b, *, layer, idx, alpha, n_ctx_tiles,
               lat_only, mode):
    bsz, t_all, d = h.shape
    k = w_mix.shape[-2]
    d_ff = w_out.shape[-2]
    off = n_ctx_tiles if lat_only else 0
    grid_t = t_all // TM - off
    if mode == "product":
        act_specs = [_tile_spec(k, off) for _ in acts]
    else:
        act_specs = [pl.BlockSpec((None, None, k, TM), lambda bi, ti: (bi, ti + off, 0, 0))]
    return pl.pallas_call(
        functools.partial(_post_kernel, alpha=alpha, mode=mode),
        grid=(bsz, grid_t),
        in_specs=[
            _tile_spec(d, off),
            _mod_spec(d, layer, n_ctx_tiles, off),
            *act_specs,
            _weight_spec((k, d), idx),
            _ln_spec(d, layer, 1), _ln_spec(d, layer, 1),
            _weight_spec((d, 2 * d_ff), layer, 1),
            _weight_spec((d_ff, d), layer, 1),
            _ln_spec(d, layer, 2), _ln_spec(d, layer, 2),
        ],
        out_specs=_tile_spec(d),
        out_shape=jax.ShapeDtypeStruct((bsz, grid_t * TM, d), F32),
        compiler_params=_cparams(2),
        name="mixer_out_ffn_" + mode,
    )(h, mods, *acts, w_mix, ln_g, ln_b, w_in, w_out, ln_g, ln_b)


def _ret_proj_kernel(h_ref, m_ref, w_ref, cos_ref, sin_ref, q_ref, k_ref, v_ref, g_ref, *, heads):
    p = _dot(_modulated(h_ref, m_ref), w_ref[...])
    d_qk = heads * RET_DK
    d_v = heads * RET_DV
    cos = cos_ref[...]
    sin = sin_ref[...]

    def rope(x):
        parts = []
        for j in range(RET_DK // 128):
            sl = slice(j * 128, (j + 1) * 128)
            xs = x[:, sl]
            parts.append(xs * cos[:, sl] + pltpu.roll(xs, 64, 1) * sin[:, sl])
        return jnp.concatenate(parts, axis=1)

    for hh in range(heads):
        sl = slice(hh * RET_DK, (hh + 1) * RET_DK)
        q_ref[:, sl] = rope(p[:, sl]).astype(BF16)
        k_ref[:, sl] = rope(p[:, d_qk + hh * RET_DK:d_qk + (hh + 1) * RET_DK]
                            * (RET_DK ** -0.5)).astype(BF16)
    v_ref[...] = p[:, 2 * d_qk:2 * d_qk + d_v].astype(BF16)
    g_ref[...] = _silu(p[:, 2 * d_qk + d_v:]).astype(BF16)


def _ret_proj_call(h, mods, w_in, cos, sin, *, layer, idx, n_ctx_tiles):
    bsz, t_all, d = h.shape
    n = w_in.shape[-1]
    heads = n // (2 * RET_DK + 2 * RET_DV)
    d_qk, d_v = heads * RET_DK, heads * RET_DV
    shape = lambda w: jax.ShapeDtypeStruct((bsz, t_all, w), BF16)
    return pl.pallas_call(
        functools.partial(_ret_proj_kernel, heads=heads),
        grid=(bsz, t_all // TM),
        in_specs=[
            _tile_spec(d), _mod_spec(d, layer, n_ctx_tiles), _weight_spec((d, n), idx),
            pl.BlockSpec((TM, RET_DK), lambda bi, ti: (ti, 0)),
            pl.BlockSpec((TM, RET_DK), lambda bi, ti: (ti, 0)),
        ],
        out_specs=[_tile_spec(d_qk), _tile_spec(d_qk), _tile_spec(d_v), _tile_spec(d_v)],
        out_shape=[shape(d_qk), shape(d_qk), shape(d_v), shape(d_v)],
        compiler_params=_cparams(2),
        name="retention_proj",
    )(h, mods, w_in, cos, sin)


def _ret_core_kernel(q_ref, k_ref, v_ref, mask_ref, qdf_ref, qdb_ref, kdf_ref, kdb_ref,
                     cdf_ref, cdb_ref, o_ref, snap_s, sf_s, sb_s, *, n_chunks, n_ctx_chunks):
    c = RET_BLOCK
    nt = (((1,), (1,)), ((), ()))
    tn = (((0,), (0,)), ((), ()))

    sb_s[...] = jnp.zeros_like(sb_s)

    def back(i, carry):
        cb = jnp.where(i < n_ctx_chunks, n_ctx_chunks - 1 - i, n_chunks - 1 - (i - n_ctx_chunks))
        rb = pl.multiple_of(cb * c, c)
        s_prev = sb_s[...]
        snap_s[cb] = s_prev.astype(BF16)
        kd = (k_ref[pl.ds(rb, c), :] * kdb_ref[...]).astype(BF16)
        sb_s[...] = s_prev * cdb_ref[...] + lax.dot_general(
            kd, v_ref[pl.ds(rb, c), :], tn, preferred_element_type=F32)
        return carry

    lax.fori_loop(0, n_chunks, back, 0, unroll=RET_UNROLL)

    sf_s[...] = jnp.zeros_like(sf_s)

    def fwd(i, carry):
        r = pl.multiple_of(i * c, c)
        q = q_ref[pl.ds(r, c), :]
        k = k_ref[pl.ds(r, c), :]
        v = v_ref[pl.ds(r, c), :]
        s = lax.dot_general(q, k, nt, preferred_element_type=F32)
        att = (s * mask_ref[...]).astype(BF16)
        s_prev = sf_s[...]
        o = (_dot(att, v) + qdf_ref[...] * _dot(q, s_prev.astype(BF16))
             + qdb_ref[...] * _dot(q, snap_s[i]))
        kd = (k * kdf_ref[...]).astype(BF16)
        sf_s[...] = s_prev * cdf_ref[...] + lax.dot_general(kd, v, tn, preferred_element_type=F32)
        mu = jnp.mean(o, axis=-1, keepdims=True)
        oc = o - mu
        var = jnp.mean(oc * oc, axis=-1, keepdims=True)
        o_ref[pl.ds(r, c), :] = (oc * lax.rsqrt(var + LN_EPS)).astype(BF16)
        return carry

    lax.fori_loop(0, n_chunks, fwd, 0, unroll=RET_UNROLL)


def _ret_tables(heads):
    c = RET_BLOCK
    hs = jnp.arange(heads, dtype=F32)
    lgf = jnp.log1p(-jnp.exp2(-5.0 - hs))
    lgb = lgf[::-1]
    pos = jnp.arange(c, dtype=F32)
    diff = pos[:, None] - pos[None, :]
    mask = jnp.where(diff >= 0,
                     jnp.exp(lgf[:, None, None] * jnp.maximum(diff, 0.0)),
                     jnp.exp(lgb[:, None, None] * jnp.maximum(-diff, 0.0)))
    wide = lambda t, w: jnp.broadcast_to(t[:, :, None], (heads, c, w))
    qdf = wide(jnp.exp(lgf[:, None] * (pos + 1.0)), RET_DV)
    qdb = wide(jnp.exp(lgb[:, None] * (c - pos)), RET_DV)
    kdf = wide(jnp.exp(lgf[:, None] * (c - 1.0 - pos)), RET_DK)
    kdb = wide(jnp.exp(lgb[:, None] * pos), RET_DK)
    cdf = jnp.broadcast_to(jnp.exp(lgf * c)[:, None, None], (heads, 1, RET_DV))
    cdb = jnp.broadcast_to(jnp.exp(lgb * c)[:, None, None], (heads, 1, RET_DV))
    return mask, qdf, qdb, kdf, kdb, cdf, cdb


def _ret_core_call(q, k, v, *, ctx_len):
    bsz, t_all, d_qk = q.shape
    heads = d_qk // RET_DK
    c = RET_BLOCK
    tables = _ret_tables(heads)
    per_head = lambda a: pl.BlockSpec((None,) + a.shape[1:], lambda hi, bi: (hi, 0, 0))
    seq = lambda w: pl.BlockSpec((None, t_all, w), lambda hi, bi: (bi, 0, hi))
    return pl.pallas_call(
        functools.partial(_ret_core_kernel, n_chunks=t_all // c, n_ctx_chunks=ctx_len // c),
        grid=(heads, bsz),
        in_specs=[seq(RET_DK), seq(RET_DK), seq(RET_DV), *[per_head(a) for a in tables]],
        out_specs=seq(RET_DV),
        out_shape=jax.ShapeDtypeStruct((bsz, t_all, heads * RET_DV), BF16),
        scratch_shapes=[
            pltpu.VMEM((t_all // c, RET_DK, RET_DV), BF16),
            pltpu.VMEM((RET_DK, RET_DV), F32),
            pltpu.VMEM((RET_DK, RET_DV), F32),
        ],
        compiler_params=_cparams(2),
        name="retention_core",
    )(q, k, v, *tables)


def _rope_tables(seq, ctx_len):
    half = RET_DK // 2
    freqs = ROPE_BASE ** (-jnp.arange(0, half, 2, dtype=F32) / half)
    pos = jnp.arange(seq)
    ar = (pos // GRID_W).astype(F32)[:, None] * freqs
    ac = (pos % GRID_W).astype(F32)[:, None] * freqs
    cos = jnp.concatenate([jnp.cos(ar), jnp.cos(ar), jnp.cos(ac), jnp.cos(ac)], axis=-1)
    sin = jnp.concatenate([-jnp.sin(ar), jnp.sin(ar), -jnp.sin(ac), jnp.sin(ac)], axis=-1)
    cos = jnp.concatenate([jnp.ones((ctx_len, RET_DK), F32), cos], axis=0)
    sin = jnp.concatenate([jnp.zeros((ctx_len, RET_DK), F32), sin], axis=0)
    return cos, sin


def _na_proj_kernel(h_ref, m_ref, w_ref, k_ref, qt_ref, vt_ref, *, d_att):
    p = _dot(_modulated(h_ref, m_ref), w_ref[...])
    qt_ref[...] = (p[:, :d_att] * (NA_HEAD_DIM ** -0.5 * LOG2E)).T.astype(BF16)
    k_ref[...] = p[:, d_att:2 * d_att].astype(BF16)
    vt_ref[...] = p[:, 2 * d_att:].T.astype(BF16)


def _na_proj_call(h, mods, w_qkv, *, layer, idx, n_ctx_tiles):
    bsz, t_all, d = h.shape
    d_att = w_qkv.shape[-1] // 3
    n_tiles = t_all // TM
    t_spec = pl.BlockSpec((None, None, d_att, TM), lambda bi, ti: (bi, ti, 0, 0))
    t_shape = jax.ShapeDtypeStruct((bsz, n_tiles, d_att, TM), BF16)
    return pl.pallas_call(
        functools.partial(_na_proj_kernel, d_att=d_att),
        grid=(bsz, n_tiles),
        in_specs=[_tile_spec(d), _mod_spec(d, layer, n_ctx_tiles), _weight_spec((d, 3 * d_att), idx)],
        out_specs=[_tile_spec(d_att), t_spec, t_spec],
        out_shape=[jax.ShapeDtypeStruct((bsz, t_all, d_att), BF16), t_shape, t_shape],
        compiler_params=_cparams(2),
        name="na_proj",
    )(h, mods, w_qkv)


def _na_attn_kernel(k_ref, qt_ref, vt_ref, bias_ref, ot_ref, *, ctx_len, n_blocks):
    hd = NA_HEAD_DIM
    n_ctx_tiles = ctx_len // TM
    row = lax.broadcasted_iota(jnp.int32, (2 * hd, 2 * TM), 0)
    lane = lax.broadcasted_iota(jnp.int32, (2 * hd, 2 * TM), 1)
    own_head = (row < hd) == (lane < TM)
    k_ctx = k_ref[0:ctx_len, :]
    ones = jnp.ones((BF16_SUBLANES, TM), BF16)

    def pv(tile, p_tile):
        return _dot(jnp.concatenate([vt_ref[tile], ones], axis=0), p_tile.astype(BF16))

    def attend(q_tile, k_tile, bias):
        qt2 = qt_ref[q_tile]
        qt = jnp.concatenate([qt2, qt2], axis=1)
        qt = jnp.where(own_head, qt, jnp.zeros_like(qt))
        s_ctx = _dot(k_ctx, qt)
        m = jnp.max(s_ctx, axis=0, keepdims=True)
        if k_tile is not None:
            k_off = pl.multiple_of(k_tile * TM, TM)
            s_loc = _dot(k_ref[pl.ds(k_off, NA_WIN_TILES * TM), :], qt) + bias
            m = jnp.maximum(m, jnp.max(s_loc, axis=0, keepdims=True))
        p_ctx = jnp.exp2(s_ctx - m)
        acc = pv(0, p_ctx[0:TM])
        for j in range(1, n_ctx_tiles):
            acc = acc + pv(j, p_ctx[j * TM:(j + 1) * TM])
        if k_tile is not None:
            p_loc = jnp.exp2(s_loc - m)
            for j in range(NA_WIN_TILES):
                acc = acc + pv(k_tile + j, p_loc[j * TM:(j + 1) * TM])
        o0 = acc[0:hd, 0:TM] * (1.0 / acc[2 * hd:2 * hd + 1, 0:TM])
        o1 = acc[hd:2 * hd, TM:2 * TM] * (1.0 / acc[2 * hd:2 * hd + 1, TM:2 * TM])
        ot_ref[q_tile] = jnp.concatenate([o0, o1], axis=0)

    for j in range(n_ctx_tiles):
        attend(j, None, None)

    def blocks(i, carry):
        for jj in range(NA_BLOCKS_PER_ITER):
            blk = i * NA_BLOCKS_PER_ITER + jj
            variant = jnp.where(blk == 0, 0, jnp.where(blk == n_blocks - 1, 2, 1))
            k_tile = n_ctx_tiles + jnp.clip(blk - 1, 0, n_blocks - NA_WIN_TILES)
            attend(n_ctx_tiles + blk, k_tile, bias_ref[variant])
        return carry

    lax.fori_loop(0, n_blocks // NA_BLOCKS_PER_ITER, blocks, 0)


def _na_bias_tables(rpb, grid_rows):
    heads = rpb.shape[0]
    w = GRID_W
    col = jnp.arange(w)
    start = jnp.clip(col - NA_WIN_C // 2, 0, w - NA_WIN_C)
    ok = (col[:, None] >= start[None, :]) & (col[:, None] < start[None, :] + NA_WIN_C)
    rel_c = jnp.clip(col[:, None] - col[None, :] + NA_WIN_C - 1, 0, 2 * NA_WIN_C - 2)
    colb = jnp.where(ok[None, None], rpb[:, :, rel_c].astype(F32) * LOG2E, NEG_INF)
    masked = jnp.full((heads, w, w), NEG_INF, F32)
    key_rows = NA_WIN_TILES * TM // w

    def build(q_row0, k_row0):
        out = []
        for kj in range(key_rows):
            blocks = []
            for qi in range(NA_ROWS_PER_BLOCK):
                r, kr = q_row0 + qi, k_row0 + kj
                rs = min(max(r - NA_WIN_R // 2, 0), grid_rows - NA_WIN_R)
                inside = rs <= kr < rs + NA_WIN_R
                blocks.append(colb[:, kr - r + NA_WIN_R - 1] if inside else masked)
            out.append(jnp.concatenate(blocks, axis=2))
        return jnp.concatenate(out, axis=1)

    rpb_blk = NA_ROWS_PER_BLOCK
    variants = jnp.stack([build(0, 0), build(rpb_blk, 0),
                          build(grid_rows - rpb_blk, grid_rows - key_rows)], axis=1)
    v = variants.reshape(heads // 2, 2, 3, key_rows * w, TM)
    return jnp.concatenate([v[:, 0], v[:, 1]], axis=-1)


def _na_attn_call(k, qt, vt, rpb, *, ctx_len):
    bsz, t_all, d_att = k.shape
    n_tiles = t_all // TM
    pairs = d_att // (2 * NA_HEAD_DIM)
    n_blocks = (t_all - ctx_len) // TM
    assert n_blocks % NA_BLOCKS_PER_ITER == 0 and n_blocks >= NA_WIN_TILES
    bias = _na_bias_tables(rpb, (t_all - ctx_len) // GRID_W)
    t_spec = pl.BlockSpec((None, n_tiles, 2 * NA_HEAD_DIM, TM), lambda pi, bi: (bi, 0, pi, 0))
    return pl.pallas_call(
        functools.partial(_na_attn_kernel, ctx_len=ctx_len, n_blocks=n_blocks),
        grid=(pairs, bsz),
        in_specs=[
            pl.BlockSpec((None, t_all, 2 * NA_HEAD_DIM), lambda pi, bi: (bi, 0, pi)),
            t_spec, t_spec,
            pl.BlockSpec((None,) + bias.shape[1:], lambda pi, bi: (pi, 0, 0, 0)),
        ],
        out_specs=t_spec,
        out_shape=jax.ShapeDtypeStruct((bsz, n_tiles, d_att, TM), F32),
        compiler_params=_cparams(2),
        name="na_attention",
    )(k, qt, vt, bias)


def _lru_proj_kernel(h_ref, m_ref, w_ref, gate_ref, x_ref, *, width):
    p = _dot(_modulated(h_ref, m_ref), w_ref[...])
    gate_ref[...] = _gelu_tanh(p[:, :width]).astype(BF16)
    x_ref[...] = p[:, width:]


def _lru_proj_call(h, mods, w_in, *, layer, idx, n_ctx_tiles):
    bsz, t_all, d = h.shape
    width = w_in.shape[-1] // 2
    return pl.pallas_call(
        functools.partial(_lru_proj_kernel, width=width),
        grid=(bsz, t_all // TM),
        in_specs=[_tile_spec(d), _mod_spec(d, layer, n_ctx_tiles), _weight_spec((d, 2 * width), idx)],
        out_specs=[_tile_spec(width), _tile_spec(width)],
        out_shape=[jax.ShapeDtypeStruct((bsz, t_all, width), BF16),
                   jax.ShapeDtypeStruct((bsz, t_all, width), F32)],
        compiler_params=_cparams(2),
        name="rglru_proj",
    )(h, mods, w_in)


def _lru_core_kernel(x_ref, cw_ref, cb_ref, wa_ref, ba_ref, wx_ref, bx_ref, lam_ref, o_ref,
                     xp_s, a_s, u_s, *, t_all, ctx_len):
    pad = SUBLANES
    ch = LRU_CHUNK
    bw = LRU_BLOCK_W
    n_chunks = t_all // ch
    n_ctx_chunks = ctx_len // ch

    zeros = jnp.zeros((pad, bw), F32)
    xp_s[0:pad, :] = zeros
    xp_s[pad + ctx_len:2 * pad + ctx_len, :] = zeros
    xp_s[2 * pad + t_all:3 * pad + t_all, :] = zeros

    def padded_row(i):
        return pl.multiple_of(i * ch + jnp.where(i >= n_ctx_chunks, 2 * pad, pad), SUBLANES)

    def copy(i, carry):
        r = pl.multiple_of(i * ch, ch)
        xp_s[pl.ds(padded_row(i), ch), :] = x_ref[pl.ds(r, ch), :]
        return carry

    lax.fori_loop(0, n_chunks, copy, 0)

    cw = cw_ref[...]
    cb = cb_ref[...]
    neg_lam = -lam_ref[...]
    softplus = jnp.maximum(neg_lam, 0.0) + jnp.log1p(jnp.exp(-jnp.abs(neg_lam)))
    decay_rate = -LRU_C * softplus

    def gates(i, carry):
        r = pl.multiple_of(i * ch, ch)
        lo = pl.multiple_of(padded_row(i) - pad, SUBLANES)
        xe = xp_s[pl.ds(lo, ch + 2 * pad), :]
        mid = slice(pad, pad + ch)
        x_m2 = pltpu.roll(xe, 2, 0)[mid]
        x_m1 = pltpu.roll(xe, 1, 0)[mid]
        x_p1 = pltpu.roll(xe, ch + 2 * pad - 1, 0)[mid]
        x = cw[0:1] * x_m2 + cw[1:2] * x_m1 + cw[2:3] * xe[mid] + cw[3:4] * x_p1 + cb
        xb = x.astype(BF16)
        for d in range(2):
            rg = _sigmoid(_dot(xb, wa_ref[d]) + ba_ref[d:d + 1])
            ig = _sigmoid(_dot(xb, wx_ref[d]) + bx_ref[d:d + 1])
            log_a = rg * decay_rate[d:d + 1]
            a_s[d, pl.ds(r, ch), :] = jnp.exp(log_a)
            th = jnp.tanh(log_a)
            m2 = -2.0 * th / (1.0 - th)
            mult = m2 * lax.rsqrt(jnp.maximum(m2, 1e-37))
            u_s[d, pl.ds(r, ch), :] = mult * (ig * x)
        return carry

    lax.fori_loop(0, n_chunks, gates, 0)

    sub = lax.broadcasted_iota(jnp.int32, (SUBLANES, bw), 0)
    rows_per_iter = SUBLANES * SCAN_GROUPS
    n_blocks = t_all // rows_per_iter
    n_ctx_blocks = ctx_len // rows_per_iter

    def scan8(a, u, reverse):
        for s in (1, 2, 4):
            if reverse:
                valid = sub < SUBLANES - s
                shift = SUBLANES - s
            else:
                valid = sub >= s
                shift = s
            u = u + a * jnp.where(valid, pltpu.roll(u, shift, 0), 0.0)
            a = a * jnp.where(valid, pltpu.roll(a, shift, 0), 1.0)
        return a, u

    def rec(i, carry):
        hf, hb = carry
        rf = pl.multiple_of(i * rows_per_iter, rows_per_iter)
        bb = jnp.where(i < n_ctx_blocks, n_ctx_blocks - 1 - i, n_blocks - 1 - (i - n_ctx_blocks))
        rb = pl.multiple_of(bb * rows_per_iter, rows_per_iter)
        af, uf = a_s[0, pl.ds(rf, rows_per_iter), :], u_s[0, pl.ds(rf, rows_per_iter), :]
        ab, ub = a_s[1, pl.ds(rb, rows_per_iter), :], u_s[1, pl.ds(rb, rows_per_iter), :]
        outs_f, outs_b = [], []
        for j in range(SCAN_GROUPS):
            g = slice(j * SUBLANES, (j + 1) * SUBLANES)
            a, u = scan8(af[g], uf[g], False)
            h = u + a * hf
            outs_f.append(h)
            hf = h[SUBLANES - 1:SUBLANES]
            g = slice((SCAN_GROUPS - 1 - j) * SUBLANES, (SCAN_GROUPS - j) * SUBLANES)
            a, u = scan8(ab[g], ub[g], True)
            h = u + a * hb
            outs_b.append(h)
            hb = h[0:1]
        u_s[0, pl.ds(rf, rows_per_iter), :] = jnp.concatenate(outs_f, axis=0)
        u_s[1, pl.ds(rb, rows_per_iter), :] = jnp.concatenate(outs_b[::-1], axis=0)
        return hf, hb

    zero = jnp.zeros((1, bw), F32)
    lax.fori_loop(0, n_blocks, rec, (zero, zero))

    def emit(i, carry):
        r = pl.multiple_of(i * ch, ch)
        o_ref[pl.ds(r, ch), :] = (u_s[0, pl.ds(r, ch), :] + u_s[1, pl.ds(r, ch), :]).astype(BF16)
        return carry

    lax.fori_loop(0, n_chunks, emit, 0)


def _lru_core_call(x, conv_w, conv_b, w_a, b_a, w_x, b_x, lam, *, idx, ctx_len):
    bsz, t_all, width = x.shape
    bw = LRU_BLOCK_W
    blocks = width // bw
    col = lambda rows: pl.BlockSpec((None, rows, bw), lambda ki, bi: (idx, 0, ki))
    wspec = pl.BlockSpec((None, 2, None, bw, bw), lambda ki, bi: (idx, 0, ki, 0, 0))
    seq = pl.BlockSpec((None, t_all, bw), lambda ki, bi: (bi, 0, ki))
    return pl.pallas_call(
        functools.partial(_lru_core_kernel, t_all=t_all, ctx_len=ctx_len),
        grid=(blocks, bsz),
        in_specs=[seq, col(LRU_CONV_W), col(1), wspec, col(2), wspec, col(2), col(2)],
        out_specs=seq,
        out_shape=jax.ShapeDtypeStruct((bsz, t_all, width), BF16),
        scratch_shapes=[
            pltpu.VMEM((t_all + 3 * SUBLANES, bw), F32),
            pltpu.VMEM((2, t_all, bw), F32),
            pltpu.VMEM((2, t_all, bw), F32),
        ],
        compiler_params=_cparams(2),
        name="rglru_core",
    )(x, conv_w, conv_b, w_a, b_a, w_x, b_x, lam)


def kernel(x, c, ctx, c_ctx, ada_w, ada_b, ln_g, ln_b, ffn_w_in, ffn_w_out, ret_w_in, ret_w_out,
           na_w_qkv, na_rpb, na_w_out, lru_w_in, lru_conv_w, lru_conv_b, lru_w_a, lru_b_a,
           lru_w_x, lru_b_x, lru_lam, lru_w_out):
    bsz, seq, d = x.shape
    ctx_len = ctx.shape[1]
    t_all = ctx_len + seq
    depth = ada_w.shape[0]
    alpha = (2 * depth) ** 0.25
    assert ctx_len % TM == 0 and seq % TM == 0 and ctx_len % LRU_CHUNK == 0
    assert ctx_len % RET_BLOCK == 0 and seq % RET_BLOCK == 0
    assert NA_ROWS_PER_BLOCK * GRID_W == TM
    assert (NA_WIN_R + NA_ROWS_PER_BLOCK) * GRID_W == NA_WIN_TILES * TM
    n_ctx_tiles = ctx_len // TM

    s = jnp.concatenate([jnp.broadcast_to(c_ctx[None, :], (bsz, d)), c], axis=0)
    mods = _mods_call(s, ada_w, ada_b).reshape(depth, 2, bsz, N_MOD, d)
    cos, sin = _rope_tables(seq, ctx_len)
    bf = lambda w: w.astype(BF16)
    ffn_w_in, ffn_w_out = bf(ffn_w_in), bf(ffn_w_out)
    ret_w_in, ret_w_out = bf(ret_w_in), bf(ret_w_out)
    na_w_qkv, na_w_out = bf(na_w_qkv), bf(na_w_out)
    lru_w_in, lru_w_out, lru_w_a, lru_w_x = bf(lru_w_in), bf(lru_w_out), bf(lru_w_a), bf(lru_w_x)
    ln_g = ln_g.reshape(depth, 3, 1, d)
    ln_b = ln_b.reshape(depth, 3, 1, d)
    lru_conv_b = lru_conv_b[:, None, :]

    h = None
    for layer in range(depth):
        last = layer == depth - 1
        kind, idx = layer % 3, layer // 3
        lay = dict(layer=layer, idx=idx, n_ctx_tiles=n_ctx_tiles)
        h = _ffn_in_call([ctx, x] if layer == 0 else [h], mods, ffn_w_in, ffn_w_out, ln_g, ln_b,
                         layer=layer, alpha=alpha, n_ctx_tiles=n_ctx_tiles, t_all=t_all)
        if kind == 0:
            q, k, v, sg = _ret_proj_call(h, mods, ret_w_in, cos, sin, **lay)
            acts, w_mix, mode = [sg, _ret_core_call(q, k, v, ctx_len=ctx_len)], ret_w_out, "product"
        elif kind == 1:
            k, qt, vt = _na_proj_call(h, mods, na_w_qkv, **lay)
            acts, w_mix, mode = [_na_attn_call(k, qt, vt, na_rpb[idx], ctx_len=ctx_len)], na_w_out, "transposed"
        else:
            gate, xr = _lru_proj_call(h, mods, lru_w_in, **lay)
            hs = _lru_core_call(xr, lru_conv_w, lru_conv_b, lru_w_a, lru_b_a, lru_w_x, lru_b_x,
                                lru_lam, idx=idx, ctx_len=ctx_len)
            acts, w_mix, mode = [gate, hs], lru_w_out, "product"
        h = _post_call(h, mods, acts, w_mix, ffn_w_in, ffn_w_out, ln_g, ln_b, alpha=alpha,
                       lat_only=last, mode=mode, **lay)
    return h
```

```python
import functools

import jax
import jax.numpy as jnp
import numpy as np
from jax import lax
from jax.experimental import pallas as pl
from jax.experimental.pallas import tpu as pltpu

F32 = jnp.float32
BF16 = jnp.bfloat16

GRID_W = 64
N_MOD = 9
FFN_RES = 0.5
RET_DK = 256
RET_DV = 512
RET_BLOCK = 256
RET_UNROLL = 4
ROPE_BASE = 10000.0
NA_HEAD_DIM = 64
NA_WIN_R = 8
NA_WIN_C = 16
NA_ROWS_PER_BLOCK = 4
NA_WIN_TILES = 3
NEG_INF = -1e30
LOG2E = 1.4426950408889634
LRU_BLOCK_W = 256
LRU_CONV_W = 4
LRU_C = 8.0
LN_EPS = 1e-5

TM = 256
LRU_CHUNK = 256
SUBLANES = 8
BF16_SUBLANES = 16
SCAN_GROUPS = 8
VMEM_LIMIT = 56 * 1024 * 1024


def _cparams(n_axes):
    return pltpu.CompilerParams(
        dimension_semantics=("arbitrary",) * n_axes, vmem_limit_bytes=VMEM_LIMIT)


def _weight_spec(tail, *lead):
    zeros = (0,) * len(tail)
    return pl.BlockSpec((None,) * len(lead) + tuple(tail), lambda bi, ti: tuple(lead) + zeros,
                        pipeline_mode=pl.Buffered(1))


def _mod_spec(d, layer, n_ctx_tiles, off=0):
    return pl.BlockSpec(
        (None, None, None, N_MOD, d),
        lambda bi, ti: (layer, jnp.where(ti + off >= n_ctx_tiles, 1, 0), bi, 0, 0))


def _ln_spec(d, layer, j):
    return pl.BlockSpec((None, None, 1, d), lambda bi, ti: (layer, j, 0, 0))


def _tile_spec(width, off=0):
    return pl.BlockSpec((None, TM, width), lambda bi, ti: (bi, ti + off, 0))


def _silu(x):
    return x * (1.0 / (1.0 + jnp.exp(-x)))


def _sigmoid(x):
    return 0.5 * jnp.tanh(0.5 * x) + 0.5


def _gelu_tanh(x):
    return 0.5 * x * (1.0 + jnp.tanh(0.7978845608028654 * (x + 0.044715 * (x * x * x))))


def _layer_norm(z, g, b):
    mu = jnp.mean(z, axis=-1, keepdims=True)
    zc = z - mu
    var = jnp.mean(zc * zc, axis=-1, keepdims=True)
    return zc * lax.rsqrt(var + LN_EPS) * g + b


def _dot(a, b):
    return jnp.dot(a, b, preferred_element_type=F32)


def _half_ffn(h, m, j0, win_ref, wout_ref, g, b, alpha):
    d_ff = wout_ref.shape[0]
    u = (h * (1.0 + m[j0 + 1:j0 + 2]) + m[j0:j0 + 1]).astype(BF16)
    gu = _dot(u, win_ref[...])
    a = (_silu(gu[:, :d_ff]) * gu[:, d_ff:]).astype(BF16)
    y = _dot(a, wout_ref[...])
    return _layer_norm(alpha * h + m[j0 + 2:j0 + 3] * (FFN_RES * y), g, b)


def _modulated(h_ref, m_ref):
    m = m_ref[...]
    return (h_ref[...] * (1.0 + m[4:5]) + m[3:4]).astype(BF16)


def _mods_kernel(s_ref, w_ref, b_ref, o_ref):
    s = _silu(s_ref[...]).astype(BF16)
    o_ref[0] = _dot(s, w_ref[0].astype(BF16)) + b_ref[0]


def _mods_call(s, ada_w, ada_b):
    depth, d, n = ada_w.shape
    tn = 1024
    rows = s.shape[0]
    return pl.pallas_call(
        _mods_kernel,
        grid=(depth, n // tn),
        in_specs=[
            pl.BlockSpec((rows, d), lambda l, j: (0, 0)),
            pl.BlockSpec((1, d, tn), lambda l, j: (l, 0, j)),
            pl.BlockSpec((1, 1, tn), lambda l, j: (l, 0, j)),
        ],
        out_specs=pl.BlockSpec((1, rows, tn), lambda l, j: (l, 0, j)),
        out_shape=jax.ShapeDtypeStruct((depth, rows, n), F32),
        compiler_params=_cparams(2),
        name="adaln_mods",
    )(s, ada_w, ada_b.reshape(depth, 1, n))


def _ffn_in_kernel(*refs, alpha, n_ctx_tiles, split):
    if split:
        c_ref, x_ref, m_ref, win_ref, wout_ref, g_ref, b_ref, o_ref = refs
        h = jnp.where(pl.program_id(1) < n_ctx_tiles, c_ref[...], x_ref[...])
    else:
        h_ref, m_ref, win_ref, wout_ref, g_ref, b_ref, o_ref = refs
        h = h_ref[...]
    o_ref[...] = _half_ffn(h, m_ref[...], 0, win_ref, wout_ref, g_ref[...], b_ref[...], alpha)


def _ffn_in_call(srcs, mods, w_in, w_out, ln_g, ln_b, *, layer, alpha, n_ctx_tiles, t_all):
    bsz, _, d = srcs[0].shape
    d_ff = w_out.shape[-2]
    split = len(srcs) == 2
    if split:
        src_specs = [
            pl.BlockSpec((None, TM, d), lambda bi, ti: (bi, jnp.minimum(ti, n_ctx_tiles - 1), 0)),
            pl.BlockSpec((None, TM, d), lambda bi, ti: (bi, jnp.maximum(ti - n_ctx_tiles, 0), 0)),
        ]
    else:
        src_specs = [_tile_spec(d)]
    return pl.pallas_call(
        functools.partial(_ffn_in_kernel, alpha=alpha, n_ctx_tiles=n_ctx_tiles, split=split),
        grid=(bsz, t_all // TM),
        in_specs=src_specs + [
            _mod_spec(d, layer, n_ctx_tiles),
            _weight_spec((d, 2 * d_ff), layer, 0),
            _weight_spec((d_ff, d), layer, 0),
            _ln_spec(d, layer, 0), _ln_spec(d, layer, 0),
        ],
        out_specs=_tile_spec(d),
        out_shape=jax.ShapeDtypeStruct((bsz, t_all, d), F32),
        compiler_params=_cparams(2),
        name="ffn_in",
    )(*srcs, mods, w_in, w_out, ln_g, ln_b)


def _post_kernel(h_ref, m_ref, *rest, alpha, mode):
    if mode == "product":
        a_ref, c_ref, wmix_ref, g1_ref, b1_ref, win_ref, wout_ref, g2_ref, b2_ref, o_ref = rest
        act = a_ref[...] * c_ref[...]
    else:
        a_ref, wmix_ref, g1_ref, b1_ref, win_ref, wout_ref, g2_ref, b2_ref, o_ref = rest
        act = a_ref[...].T.astype(BF16)
    m = m_ref[...]
    h = _layer_norm(alpha * h_ref[...] + m[5:6] * _dot(act, wmix_ref[...]), g1_ref[...], b1_ref[...])
    o_ref[...] = _half_ffn(h, m, 6, win_ref, wout_ref, g2_ref[...], b2_ref[...], alpha)


def _post_call(h, mods, acts, w_mix, w_in, w_out, ln_g, ln_b, *, layer, idx, alpha, n_ctx_tiles,
               lat_only, mode):
    bsz, t_all, d = h.shape
    k = w_mix.shape[-2]
    d_ff = w_out.shape[-2]
    off = n_ctx_tiles if lat_only else 0
    grid_t = t_all // TM - off
    if mode == "product":
        act_specs = [_tile_spec(k, off) for _ in acts]
    else:
        act_specs = [pl.BlockSpec((None, None, k, TM), lambda bi, ti: (bi, ti + off, 0, 0))]
    return pl.pallas_call(
        functools.partial(_post_kernel, alpha=alpha, mode=mode),
        grid=(bsz, grid_t),
        in_specs=[
            _tile_spec(d, off),
            _mod_spec(d, layer, n_ctx_tiles, off),
            *act_specs,
            _weight_spec((k, d), idx),
            _ln_spec(d, layer, 1), _ln_spec(d, layer, 1),
            _weight_spec((d, 2 * d_ff), layer, 1),
            _weight_spec((d_ff, d), layer, 1),
            _ln_spec(d, layer, 2), _ln_spec(d, layer, 2),
        ],
        out_specs=_tile_spec(d),
        out_shape=jax.ShapeDtypeStruct((bsz, grid_t * TM, d), F32),
        compiler_params=_cparams(2),
        name="mixer_out_ffn_" + mode,
    )(h, mods, *acts, w_mix, ln_g, ln_b, w_in, w_out, ln_g, ln_b)


def _ret_proj_kernel(h_ref, m_ref, w_ref, cos_ref, sin_ref, q_ref, k_ref, v_ref, g_ref, *, heads):
    p = _dot(_modulated(h_ref, m_ref), w_ref[...])
    d_qk = heads * RET_DK
    d_v = heads * RET_DV
    cos = cos_ref[...]
    sin = sin_ref[...]

    def rope(x):
        parts = []
        for j in range(RET_DK // 128):
            sl = slice(j * 128, (j + 1) * 128)
            xs = x[:, sl]
            parts.append(xs * cos[:, sl] + pltpu.roll(xs, 64, 1) * sin[:, sl])
        return jnp.concatenate(parts, axis=1)

    for hh in range(heads):
        sl = slice(hh * RET_DK, (hh + 1) * RET_DK)
        q_ref[:, sl] = rope(p[:, sl]).astype(BF16)
        k_ref[:, sl] = rope(p[:, d_qk + hh * RET_DK:d_qk + (hh + 1) * RET_DK]
                            * (RET_DK ** -0.5)).astype(BF16)
    v_ref[...] = p[:, 2 * d_qk:2 * d_qk + d_v].astype(BF16)
    g_ref[...] = _silu(p[:, 2 * d_qk + d_v:]).astype(BF16)


def _ret_proj_call(h, mods, w_in, cos, sin, *, layer, idx, n_ctx_tiles):
    bsz, t_all, d = h.shape
    n = w_in.shape[-1]
    heads = n // (2 * RET_DK + 2 * RET_DV)
    d_qk, d_v = heads * RET_DK, heads * RET_DV
    shape = lambda w: jax.ShapeDtypeStruct((bsz, t_all, w), BF16)
    return pl.pallas_call(
        functools.partial(_ret_proj_kernel, heads=heads),
        grid=(bsz, t_all // TM),
        in_specs=[
            _tile_spec(d), _mod_spec(d, layer, n_ctx_tiles), _weight_spec((d, n), idx),
            pl.BlockSpec((TM, RET_DK), lambda bi, ti: (ti, 0)),
            pl.BlockSpec((TM, RET_DK), lambda bi, ti: (ti, 0)),
        ],
        out_specs=[_tile_spec(d_qk), _tile_spec(d_qk), _tile_spec(d_v), _tile_spec(d_v)],
        out_shape=[shape(d_qk), shape(d_qk), shape(d_v), shape(d_v)],
        compiler_params=_cparams(2),
        name="retention_proj",
    )(h, mods, w_in, cos, sin)


def _ret_core_kernel(q_ref, k_ref, v_ref, mask_ref, qdf_ref, qdb_ref, kdf_ref, kdb_ref,
                     cdf_ref, cdb_ref, o_ref, snap_s, sf_s, sb_s, *, n_chunks, n_ctx_chunks):
    c = RET_BLOCK
    nt = (((1,), (1,)), ((), ()))
    tn = (((0,), (0,)), ((), ()))

    sb_s[...] = jnp.zeros_like(sb_s)

    def back(i, carry):
        cb = jnp.where(i < n_ctx_chunks, n_ctx_chunks - 1 - i, n_chunks - 1 - (i - n_ctx_chunks))
        rb = pl.multiple_of(cb * c, c)
        s_prev = sb_s[...]
        snap_s[cb] = s_prev.astype(BF16)
        kd = (k_ref[pl.ds(rb, c), :] * kdb_ref[...]).astype(BF16)
        sb_s[...] = s_prev * cdb_ref[...] + lax.dot_general(
            kd, v_ref[pl.ds(rb, c), :], tn, preferred_element_type=F32)
        return carry

    lax.fori_loop(0, n_chunks, back, 0, unroll=RET_UNROLL)

    sf_s[...] = jnp.zeros_like(sf_s)

    def fwd(i, carry):
        r = pl.multiple_of(i * c, c)
        q = q_ref[pl.ds(r, c), :]
        k = k_ref[pl.ds(r, c), :]
        v = v_ref[pl.ds(r, c), :]
        s = lax.dot_general(q, k, nt, preferred_element_type=F32)
        att = (s * mask_ref[...]).astype(BF16)
        s_prev = sf_s[...]
        o = (_dot(att, v) + qdf_ref[...] * _dot(q, s_prev.astype(BF16))
             + qdb_ref[...] * _dot(q, snap_s[i]))
        kd = (k * kdf_ref[...]).astype(BF16)
        sf_s[...] = s_prev * cdf_ref[...] + lax.dot_general(kd, v, tn, preferred_element_type=F32)
        mu = jnp.mean(o, axis=-1, keepdims=True)
        oc = o - mu
        var = jnp.mean(oc * oc, axis=-1, keepdims=True)
        o_ref[pl.ds(r, c), :] = (oc * lax.rsqrt(var + LN_EPS)).astype(BF16)
        return carry

    lax.fori_loop(0, n_chunks, fwd, 0, unroll=RET_UNROLL)


def _ret_tables(heads):
    c = RET_BLOCK
    hs = np.arange(heads, dtype=np.float64)
    lgf = np.log1p(-np.exp2(-5.0 - hs))
    lgb = lgf[::-1]
    pos = np.arange(c, dtype=np.float64)
    diff = pos[:, None] - pos[None, :]
    mask = np.where(diff >= 0,
                    np.exp(lgf[:, None, None] * np.maximum(diff, 0.0)),
                    np.exp(lgb[:, None, None] * np.maximum(-diff, 0.0)))
    wide = lambda t, w: np.broadcast_to(t[:, :, None], (heads, c, w))
    qdf = wide(np.exp(lgf[:, None] * (pos + 1.0)), RET_DV)
    qdb = wide(np.exp(lgb[:, None] * (c - pos)), RET_DV)
    kdf = wide(np.exp(lgf[:, None] * (c - 1.0 - pos)), RET_DK)
    kdb = wide(np.exp(lgb[:, None] * pos), RET_DK)
    cdf = np.broadcast_to(np.exp(lgf * c)[:, None, None], (heads, 1, RET_DV))
    cdb = np.broadcast_to(np.exp(lgb * c)[:, None, None], (heads, 1, RET_DV))
    return tuple(np.ascontiguousarray(t, dtype=np.float32) for t in (mask, qdf, qdb, kdf, kdb, cdf, cdb))


def _ret_core_call(q, k, v, *, ctx_len):
    bsz, t_all, d_qk = q.shape
    heads = d_qk // RET_DK
    c = RET_BLOCK
    tables = _ret_tables(heads)
    per_head = lambda a: pl.BlockSpec((None,) + a.shape[1:], lambda hi, bi: (hi, 0, 0))
    seq = lambda w: pl.BlockSpec((None, t_all, w), lambda hi, bi: (bi, 0, hi))
    return pl.pallas_call(
        functools.partial(_ret_core_kernel, n_chunks=t_all // c, n_ctx_chunks=ctx_len // c),
        grid=(heads, bsz),
        in_specs=[seq(RET_DK), seq(RET_DK), seq(RET_DV), *[per_head(a) for a in tables]],
        out_specs=seq(RET_DV),
        out_shape=jax.ShapeDtypeStruct((bsz, t_all, heads * RET_DV), BF16),
        scratch_shapes=[
            pltpu.VMEM((t_all // c, RET_DK, RET_DV), BF16),
            pltpu.VMEM((RET_DK, RET_DV), F32),
            pltpu.VMEM((RET_DK, RET_DV), F32),
        ],
        compiler_params=_cparams(2),
        name="retention_core",
    )(q, k, v, *tables)


def _rope_tables(seq, ctx_len):
    half = RET_DK // 2
    freqs = ROPE_BASE ** (-np.arange(0, half, 2, dtype=np.float64) / half)
    pos = np.arange(seq)
    ar = (pos // GRID_W).astype(np.float64)[:, None] * freqs
    ac = (pos % GRID_W).astype(np.float64)[:, None] * freqs
    cos = np.concatenate([np.cos(ar), np.cos(ar), np.cos(ac), np.cos(ac)], axis=-1)
    sin = np.concatenate([-np.sin(ar), np.sin(ar), -np.sin(ac), np.sin(ac)], axis=-1)
    cos = np.concatenate([np.ones((ctx_len, RET_DK)), cos], axis=0)
    sin = np.concatenate([np.zeros((ctx_len, RET_DK)), sin], axis=0)
    return cos.astype(np.float32), sin.astype(np.float32)


def _na_proj_kernel(h_ref, m_ref, w_ref, k_ref, qt_ref, vt_ref, *, d_att):
    p = _dot(_modulated(h_ref, m_ref), w_ref[...])
    qt_ref[...] = (p[:, :d_att] * (NA_HEAD_DIM ** -0.5 * LOG2E)).T.astype(BF16)
    k_ref[...] = p[:, d_att:2 * d_att].astype(BF16)
    vt_ref[...] = p[:, 2 * d_att:].T.astype(BF16)


def _na_proj_call(h, mods, w_qkv, *, layer, idx, n_ctx_tiles):
    bsz, t_all, d = h.shape
    d_att = w_qkv.shape[-1] // 3
    n_tiles = t_all // TM
    t_spec = pl.BlockSpec((None, None, d_att, TM), lambda bi, ti: (bi, ti, 0, 0))
    t_shape = jax.ShapeDtypeStruct((bsz, n_tiles, d_att, TM), BF16)
    return pl.pallas_call(
        functools.partial(_na_proj_kernel, d_att=d_att),
        grid=(bsz, n_tiles),
        in_specs=[_tile_spec(d), _mod_spec(d, layer, n_ctx_tiles), _weight_spec((d, 3 * d_att), idx)],
        out_specs=[_tile_spec(d_att), t_spec, t_spec],
        out_shape=[jax.ShapeDtypeStruct((bsz, t_all, d_att), BF16), t_shape, t_shape],
        compiler_params=_cparams(2),
        name="na_proj",
    )(h, mods, w_qkv)


def _na_attn_kernel(k_ref, qt_ref, vt_ref, bias_ref, ot_ref, s0_s, s1_s, *, ctx_len, n_blocks):
    hd = NA_HEAD_DIM
    n_ctx_tiles = ctx_len // TM
    row = lax.broadcasted_iota(jnp.int32, (2 * hd, 2 * TM), 0)
    lane = lax.broadcasted_iota(jnp.int32, (2 * hd, 2 * TM), 1)
    own_head = (row < hd) == (lane < TM)
    k_ctx = k_ref[0:ctx_len, :]
    ones = jnp.ones((BF16_SUBLANES, TM), BF16)

    def pv(tile, p_tile):
        return _dot(jnp.concatenate([vt_ref[tile], ones], axis=0), p_tile.astype(BF16))

    def queries(q_tile):
        qt2 = qt_ref[q_tile]
        qt = jnp.concatenate([qt2, qt2], axis=1)
        return jnp.where(own_head, qt, jnp.zeros_like(qt))

    def probs(s, m):
        return jnp.exp2((s - m).astype(BF16))

    def write_out(q_tile, acc):
        o0 = acc[0:hd, 0:TM] * (1.0 / acc[2 * hd:2 * hd + 1, 0:TM])
        o1 = acc[hd:2 * hd, TM:2 * TM] * (1.0 / acc[2 * hd:2 * hd + 1, TM:2 * TM])
        ot_ref[q_tile] = jnp.concatenate([o0, o1], axis=0)

    for t in range(n_ctx_tiles):
        s_ctx = _dot(k_ctx, queries(t))
        p_ctx = probs(s_ctx, jnp.max(s_ctx, axis=0, keepdims=True))
        acc = pv(0, p_ctx[0:TM])
        for j in range(1, n_ctx_tiles):
            acc = acc + pv(j, p_ctx[j * TM:(j + 1) * TM])
        write_out(t, acc)

    def key_tile(blk):
        return n_ctx_tiles + jnp.clip(blk - 1, 0, n_blocks - NA_WIN_TILES)

    def scores(blk, s_s):
        qt = queries(n_ctx_tiles + blk)
        variant = jnp.where(blk == 0, 0, jnp.where(blk == n_blocks - 1, 2, 1))
        k_off = pl.multiple_of(key_tile(blk) * TM, TM)
        s_ctx = _dot(k_ctx, qt)
        s_loc = _dot(k_ref[pl.ds(k_off, NA_WIN_TILES * TM), :], qt) + bias_ref[variant]
        s_s[0:ctx_len, :] = s_ctx
        s_s[ctx_len:ctx_len + NA_WIN_TILES * TM, :] = s_loc
        return jnp.maximum(jnp.max(s_ctx, axis=0, keepdims=True), jnp.max(s_loc, axis=0, keepdims=True))

    def output(blk, s_s, m):
        k_tile = key_tile(blk)
        acc = None
        for j in range(n_ctx_tiles + NA_WIN_TILES):
            p = probs(s_s[j * TM:(j + 1) * TM, :], m)
            part = pv(j if j < n_ctx_tiles else k_tile + (j - n_ctx_tiles), p)
            acc = part if acc is None else acc + part
        write_out(n_ctx_tiles + blk, acc)

    def step(j, m_even):
        m_odd = scores(2 * j + 1, s1_s)
        output(2 * j, s0_s, m_even)
        m_even = scores(2 * j + 2, s0_s)
        output(2 * j + 1, s1_s, m_odd)
        return m_even

    m_even = lax.fori_loop(0, n_blocks // 2 - 1, step, scores(0, s0_s))
    m_odd = scores(n_blocks - 1, s1_s)
    output(n_blocks - 2, s0_s, m_even)
    output(n_blocks - 1, s1_s, m_odd)


def _na_bias_tables(rpb, grid_rows):
    heads = rpb.shape[0]
    w = GRID_W
    col = jnp.arange(w)
    start = jnp.clip(col - NA_WIN_C // 2, 0, w - NA_WIN_C)
    ok = (col[:, None] >= start[None, :]) & (col[:, None] < start[None, :] + NA_WIN_C)
    rel_c = jnp.clip(col[:, None] - col[None, :] + NA_WIN_C - 1, 0, 2 * NA_WIN_C - 2)
    colb = jnp.where(ok[None, None], rpb[:, :, rel_c].astype(F32) * LOG2E, NEG_INF)
    masked = jnp.full((heads, w, w), NEG_INF, F32)
    key_rows = NA_WIN_TILES * TM // w

    def build(q_row0, k_row0):
        out = []
        for kj in range(key_rows):
            blocks = []
            for qi in range(NA_ROWS_PER_BLOCK):
                r, kr = q_row0 + qi, k_row0 + kj
                rs = min(max(r - NA_WIN_R // 2, 0), grid_rows - NA_WIN_R)
                inside = rs <= kr < rs + NA_WIN_R
                blocks.append(colb[:, kr - r + NA_WIN_R - 1] if inside else masked)
            out.append(jnp.concatenate(blocks, axis=2))
        return jnp.concatenate(out, axis=1)

    rpb_blk = NA_ROWS_PER_BLOCK
    variants = jnp.stack([build(0, 0), build(rpb_blk, 0),
                          build(grid_rows - rpb_blk, grid_rows - key_rows)], axis=1)
    v = variants.reshape(heads // 2, 2, 3, key_rows * w, TM)
    return jnp.concatenate([v[:, 0], v[:, 1]], axis=-1)


def _na_attn_call(k, qt, vt, rpb, *, ctx_len):
    bsz, t_all, d_att = k.shape
    n_tiles = t_all // TM
    pairs = d_att // (2 * NA_HEAD_DIM)
    n_blocks = (t_all - ctx_len) // TM
    assert n_blocks >= NA_WIN_TILES and n_blocks % 2 == 0
    bias = _na_bias_tables(rpb, (t_all - ctx_len) // GRID_W)
    t_spec = pl.BlockSpec((None, n_tiles, 2 * NA_HEAD_DIM, TM), lambda pi, bi: (bi, 0, pi, 0))
    return pl.pallas_call(
        functools.partial(_na_attn_kernel, ctx_len=ctx_len, n_blocks=n_blocks),
        grid=(pairs, bsz),
        in_specs=[
            pl.BlockSpec((None, t_all, 2 * NA_HEAD_DIM), lambda pi, bi: (bi, 0, pi)),
            t_spec, t_spec,
            pl.BlockSpec((None,) + bias.shape[1:], lambda pi, bi: (pi, 0, 0, 0)),
        ],
        out_specs=t_spec,
        out_shape=jax.ShapeDtypeStruct((bsz, n_tiles, d_att, TM), F32),
        scratch_shapes=[pltpu.VMEM((ctx_len + NA_WIN_TILES * TM, 2 * TM), F32)] * 2,
        compiler_params=_cparams(2),
        name="na_attention",
    )(k, qt, vt, bias)


def _lru_proj_kernel(h_ref, m_ref, w_ref, gate_ref, x_ref, *, width):
    p = _dot(_modulated(h_ref, m_ref), w_ref[...])
    gate_ref[...] = _gelu_tanh(p[:, :width]).astype(BF16)
    x_ref[...] = p[:, width:]


def _lru_proj_call(h, mods, w_in, *, layer, idx, n_ctx_tiles):
    bsz, t_all, d = h.shape
    width = w_in.shape[-1] // 2
    return pl.pallas_call(
        functools.partial(_lru_proj_kernel, width=width),
        grid=(bsz, t_all // TM),
        in_specs=[_tile_spec(d), _mod_spec(d, layer, n_ctx_tiles), _weight_spec((d, 2 * width), idx)],
        out_specs=[_tile_spec(width), _tile_spec(width)],
        out_shape=[jax.ShapeDtypeStruct((bsz, t_all, width), BF16),
                   jax.ShapeDtypeStruct((bsz, t_all, width), F32)],
        compiler_params=_cparams(2),
        name="rglru_proj",
    )(h, mods, w_in)


def _lru_core_kernel(x_ref, cw_ref, cb_ref, wa_ref, ba_ref, wx_ref, bx_ref, lam_ref, o_ref,
                     xp_s, a_s, u_s, *, t_all, ctx_len):
    pad = SUBLANES
    ch = LRU_CHUNK
    bw = LRU_BLOCK_W
    n_chunks = t_all // ch
    n_ctx_chunks = ctx_len // ch

    zeros = jnp.zeros((pad, bw), F32)
    xp_s[0:pad, :] = zeros
    xp_s[pad + ctx_len:2 * pad + ctx_len, :] = zeros
    xp_s[2 * pad + t_all:3 * pad + t_all, :] = zeros

    def padded_row(i):
        return pl.multiple_of(i * ch + jnp.where(i >= n_ctx_chunks, 2 * pad, pad), SUBLANES)

    def copy(i, carry):
        r = pl.multiple_of(i * ch, ch)
        xp_s[pl.ds(padded_row(i), ch), :] = x_ref[pl.ds(r, ch), :]
        return carry

    lax.fori_loop(0, n_chunks, copy, 0)

    cw = cw_ref[...]
    cb = cb_ref[...]
    neg_lam = -lam_ref[...]
    softplus = jnp.maximum(neg_lam, 0.0) + jnp.log1p(jnp.exp(-jnp.abs(neg_lam)))
    decay_rate = -LRU_C * softplus

    def gates(i, carry):
        r = pl.multiple_of(i * ch, ch)
        lo = pl.multiple_of(padded_row(i) - pad, SUBLANES)
        xe = xp_s[pl.ds(lo, ch + 2 * pad), :]
        mid = slice(pad, pad + ch)
        x_m2 = pltpu.roll(xe, 2, 0)[mid]
        x_m1 = pltpu.roll(xe, 1, 0)[mid]
        x_p1 = pltpu.roll(xe, ch + 2 * pad - 1, 0)[mid]
        x = cw[0:1] * x_m2 + cw[1:2] * x_m1 + cw[2:3] * xe[mid] + cw[3:4] * x_p1 + cb
        xb = x.astype(BF16)
        for d in range(2):
            rg = _sigmoid(_dot(xb, wa_ref[d]) + ba_ref[d:d + 1])
            ig = _sigmoid(_dot(xb, wx_ref[d]) + bx_ref[d:d + 1])
            log_a = rg * decay_rate[d:d + 1]
            a_s[d, pl.ds(r, ch), :] = jnp.exp(log_a)
            th = jnp.tanh(log_a)
            m2 = -2.0 * th / (1.0 - th)
            mult = m2 * lax.rsqrt(jnp.maximum(m2, 1e-37))
            u_s[d, pl.ds(r, ch), :] = mult * (ig * x)
        return carry

    lax.fori_loop(0, n_chunks, gates, 0)

    sub = lax.broadcasted_iota(jnp.int32, (SUBLANES, bw), 0)
    rows_per_iter = SUBLANES * SCAN_GROUPS
    n_blocks = t_all // rows_per_iter
    n_ctx_blocks = ctx_len // rows_per_iter

    def scan8(a, u, reverse):
        for s in (1, 2, 4):
            if reverse:
                valid = sub < SUBLANES - s
                shift = SUBLANES - s
            else:
                valid = sub >= s
                shift = s
            u = u + a * jnp.where(valid, pltpu.roll(u, shift, 0), 0.0)
            a = a * jnp.where(valid, pltpu.roll(a, shift, 0), 1.0)
        return a, u

    def rec(i, carry):
        hf, hb = carry
        rf = pl.multiple_of(i * rows_per_iter, rows_per_iter)
        bb = jnp.where(i < n_ctx_blocks, n_ctx_blocks - 1 - i, n_blocks - 1 - (i - n_ctx_blocks))
        rb = pl.multiple_of(bb * rows_per_iter, rows_per_iter)
        af, uf = a_s[0, pl.ds(rf, rows_per_iter), :], u_s[0, pl.ds(rf, rows_per_iter), :]
        ab, ub = a_s[1, pl.ds(rb, rows_per_iter), :], u_s[1, pl.ds(rb, rows_per_iter), :]
        outs_f, outs_b = [], []
        for j in range(SCAN_GROUPS):
            g = slice(j * SUBLANES, (j + 1) * SUBLANES)
            a, u = scan8(af[g], uf[g], False)
            h = u + a * hf
            outs_f.append(h)
            hf = h[SUBLANES - 1:SUBLANES]
            g = slice((SCAN_GROUPS - 1 - j) * SUBLANES, (SCAN_GROUPS - j) * SUBLANES)
            a, u = scan8(ab[g], ub[g], True)
            h = u + a * hb
            outs_b.append(h)
            hb = h[0:1]
        u_s[0, pl.ds(rf, rows_per_iter), :] = jnp.concatenate(outs_f, axis=0)
        u_s[1, pl.ds(rb, rows_per_iter), :] = jnp.concatenate(outs_b[::-1], axis=0)
        return hf, hb

    zero = jnp.zeros((1, bw), F32)
    lax.fori_loop(0, n_blocks, rec, (zero, zero))

    def emit(i, carry):
        r = pl.multiple_of(i * ch, ch)
        o_ref[pl.ds(r, ch), :] = (u_s[0, pl.ds(r, ch), :] + u_s[1, pl.ds(r, ch), :]).astype(BF16)
        return carry

    lax.fori_loop(0, n_chunks, emit, 0)


def _lru_core_call(x, conv_w, conv_b, w_a, b_a, w_x, b_x, lam, *, idx, ctx_len):
    bsz, t_all, width = x.shape
    bw = LRU_BLOCK_W
    blocks = width // bw
    col = lambda rows: pl.BlockSpec((None, rows, bw), lambda ki, bi: (idx, 0, ki))
    wspec = pl.BlockSpec((None, 2, None, bw, bw), lambda ki, bi: (idx, 0, ki, 0, 0))
    seq = pl.BlockSpec((None, t_all, bw), lambda ki, bi: (bi, 0, ki))
    return pl.pallas_call(
        functools.partial(_lru_core_kernel, t_all=t_all, ctx_len=ctx_len),
        grid=(blocks, bsz),
        in_specs=[seq, col(LRU_CONV_W), col(1), wspec, col(2), wspec, col(2), col(2)],
        out_specs=seq,
        out_shape=jax.ShapeDtypeStruct((bsz, t_all, width), BF16),
        scratch_shapes=[
            pltpu.VMEM((t_all + 3 * SUBLANES, bw), F32),
            pltpu.VMEM((2, t_all, bw), F32),
            pltpu.VMEM((2, t_all, bw), F32),
        ],
        compiler_params=_cparams(2),
        name="rglru_core",
    )(x, conv_w, conv_b, w_a, b_a, w_x, b_x, lam)


def kernel(x, c, ctx, c_ctx, ada_w, ada_b, ln_g, ln_b, ffn_w_in, ffn_w_out, ret_w_in, ret_w_out,
           na_w_qkv, na_rpb, na_w_out, lru_w_in, lru_conv_w, lru_conv_b, lru_w_a, lru_b_a,
           lru_w_x, lru_b_x, lru_lam, lru_w_out):
    bsz, seq, d = x.shape
    ctx_len = ctx.shape[1]
    t_all = ctx_len + seq
    depth = ada_w.shape[0]
    alpha = (2 * depth) ** 0.25
    assert ctx_len % TM == 0 and seq % TM == 0 and ctx_len % LRU_CHUNK == 0
    assert ctx_len % RET_BLOCK == 0 and seq % RET_BLOCK == 0
    assert NA_ROWS_PER_BLOCK * GRID_W == TM
    assert (NA_WIN_R + NA_ROWS_PER_BLOCK) * GRID_W == NA_WIN_TILES * TM
    n_ctx_tiles = ctx_len // TM

    s = jnp.concatenate([jnp.broadcast_to(c_ctx[None, :], (bsz, d)), c], axis=0)
    mods = _mods_call(s, ada_w, ada_b).reshape(depth, 2, bsz, N_MOD, d)
    cos, sin = _rope_tables(seq, ctx_len)
    bf = lambda w: w.astype(BF16)
    ffn_w_in, ffn_w_out = bf(ffn_w_in), bf(ffn_w_out)
    ret_w_in, ret_w_out = bf(ret_w_in), bf(ret_w_out)
    na_w_qkv, na_w_out = bf(na_w_qkv), bf(na_w_out)
    lru_w_in, lru_w_out, lru_w_a, lru_w_x = bf(lru_w_in), bf(lru_w_out), bf(lru_w_a), bf(lru_w_x)
    ln_g = ln_g.reshape(depth, 3, 1, d)
    ln_b = ln_b.reshape(depth, 3, 1, d)
    lru_conv_b = lru_conv_b[:, None, :]

    h = None
    for layer in range(depth):
        last = layer == depth - 1
        kind, idx = layer % 3, layer // 3
        lay = dict(layer=layer, idx=idx, n_ctx_tiles=n_ctx_tiles)
        h = _ffn_in_call([ctx, x] if layer == 0 else [h], mods, ffn_w_in, ffn_w_out, ln_g, ln_b,
                         layer=layer, alpha=alpha, n_ctx_tiles=n_ctx_tiles, t_all=t_all)
        if kind == 0:
            q, k, v, sg = _ret_proj_call(h, mods, ret_w_in, cos, sin, **lay)
            acts, w_mix, mode = [sg, _ret_core_call(q, k, v, ctx_len=ctx_len)], ret_w_out, "product"
        elif kind == 1:
            k, qt, vt = _na_proj_call(h, mods, na_w_qkv, **lay)
            acts, w_mix, mode = [_na_attn_call(k, qt, vt, na_rpb[idx], ctx_len=ctx_len)], na_w_out, "transposed"
        else:
            gate, xr = _lru_proj_call(h, mods, lru_w_in, **lay)
            hs = _lru_core_call(xr, lru_conv_w, lru_conv_b, lru_w_a, lru_b_a, lru_w_x, lru_b_x,
                                lru_lam, idx=idx, ctx_len=ctx_len)
            acts, w_mix, mode = [gate, hs], lru_w_out, "product"
        h = _post_call(h, mods, acts, w_mix, ffn_w_in, ffn_w_out, ln_g, ln_b, alpha=alpha,
                       lat_only=last, mode=mode, **lay)
    return h
```

```python
import functools

import jax
import jax.numpy as jnp
import numpy as np
from jax import lax
from jax.experimental import pallas as pl
from jax.experimental.pallas import tpu as pltpu

F32 = jnp.float32
BF16 = jnp.bfloat16

GRID_W = 64
N_MOD = 9
FFN_RES = 0.5
RET_DK = 256
RET_DV = 512
RET_BLOCK = 256
RET_UNROLL = 4
ROPE_BASE = 10000.0
NA_HEAD_DIM = 64
NA_WIN_R = 8
NA_WIN_C = 16
NA_ROWS_PER_BLOCK = 4
NA_WIN_TILES = 3
NEG_INF = -1e30
LOG2E = 1.4426950408889634
LRU_BLOCK_W = 256
LRU_CONV_W = 4
LRU_C = 8.0
LN_EPS = 1e-5

TM = 256
MAX_TILES_PER_STEP = 4
LRU_CHUNK = 256
SUBLANES = 8
BF16_SUBLANES = 16
SCAN_GROUPS = 8
VMEM_LIMIT = 56 * 1024 * 1024
FFN_TEMPORARIES = 8 * 1024 * 1024


def _cparams(n_axes):
    return pltpu.CompilerParams(
        dimension_semantics=("arbitrary",) * n_axes, vmem_limit_bytes=VMEM_LIMIT)


def _weight_spec(tail, *lead):
    zeros = (0,) * len(tail)
    return pl.BlockSpec((None,) * len(lead) + tuple(tail), lambda *_: tuple(lead) + zeros,
                        pipeline_mode=pl.Buffered(1))


def _mod_spec(d, layer, n_ctx_tiles, off=0, tile_of=None):
    def index(*idx):
        bi, ti = tile_of(*idx) if tile_of else idx
        return layer, jnp.where(ti + off >= n_ctx_tiles, 1, 0), bi, 0, 0

    return pl.BlockSpec((None, None, None, N_MOD, d), index)


def _ln_spec(d, layer, j):
    return pl.BlockSpec((None, None, 1, d), lambda *_: (layer, j, 0, 0))


def _tile_spec(width, off=0):
    return pl.BlockSpec((None, TM, width), lambda bi, ti: (bi, ti + off, 0))


def _silu(x):
    return x * (1.0 / (1.0 + jnp.exp(-x)))


def _sigmoid(x):
    return 0.5 * jnp.tanh(0.5 * x) + 0.5


def _gelu_tanh(x):
    return 0.5 * x * (1.0 + jnp.tanh(0.7978845608028654 * (x + 0.044715 * (x * x * x))))


def _layer_norm(z, g, b):
    mu = jnp.mean(z, axis=-1, keepdims=True)
    zc = z - mu
    var = jnp.mean(zc * zc, axis=-1, keepdims=True)
    return zc * lax.rsqrt(var + LN_EPS) * g + b


def _dot(a, b):
    return jnp.dot(a, b, preferred_element_type=F32)


def _half_ffn(h, m, j0, win_ref, wout_ref, g, b, alpha):
    d_ff = wout_ref.shape[0]
    u = (h * (1.0 + m[j0 + 1:j0 + 2]) + m[j0:j0 + 1]).astype(BF16)
    gu = _dot(u, win_ref[...])
    a = (_silu(gu[:, :d_ff]) * gu[:, d_ff:]).astype(BF16)
    y = _dot(a, wout_ref[...])
    return _layer_norm(alpha * h + m[j0 + 2:j0 + 3] * (FFN_RES * y), g, b)


def _modulated(h_ref, m_ref):
    m = m_ref[...]
    return (h_ref[...] * (1.0 + m[4:5]) + m[3:4]).astype(BF16)


def _mods_kernel(s_ref, w_ref, b_ref, o_ref):
    s = _silu(s_ref[...]).astype(BF16)
    o_ref[0] = _dot(s, w_ref[0].astype(BF16)) + b_ref[0]


def _mods_call(s, ada_w, ada_b):
    depth, d, n = ada_w.shape
    tn = 1024
    rows = s.shape[0]
    return pl.pallas_call(
        _mods_kernel,
        grid=(depth, n // tn),
        in_specs=[
            pl.BlockSpec((rows, d), lambda l, j: (0, 0)),
            pl.BlockSpec((1, d, tn), lambda l, j: (l, 0, j)),
            pl.BlockSpec((1, 1, tn), lambda l, j: (l, 0, j)),
        ],
        out_specs=pl.BlockSpec((1, rows, tn), lambda l, j: (l, 0, j)),
        out_shape=jax.ShapeDtypeStruct((depth, rows, n), F32),
        compiler_params=_cparams(2),
        name="adaln_mods",
    )(s, ada_w, ada_b.reshape(depth, 1, n))


def _tiles_per_step(n_tiles, resident_bytes, tile_io_bytes):
    n = MAX_TILES_PER_STEP
    while n > 1 and (n_tiles % n or
                     resident_bytes + 2 * n * tile_io_bytes + FFN_TEMPORARIES > VMEM_LIMIT):
        n //= 2
    return n


def _step_tile(n, tiles_per_sample, j, first_tile=0):
    def tile(p):
        g = n * p + j
        return g // tiles_per_sample, g % tiles_per_sample + first_tile

    return tile


def _ffn_in_kernel(*refs, alpha, n_ctx_tiles, tiles_per_sample, split, n):
    n_src = 2 * n if split else n
    srcs, m_refs = refs[:n_src], refs[n_src:n_src + n]
    win_ref, wout_ref, g_ref, b_ref, o_ref = refs[n_src + n:]
    for j, m_ref in enumerate(m_refs):
        if split:
            tile = (n * pl.program_id(0) + j) % tiles_per_sample
            h = jnp.where(tile < n_ctx_tiles, srcs[2 * j][...], srcs[2 * j + 1][...])
        else:
            h = srcs[j][...]
        o_ref[j * TM:(j + 1) * TM, :] = _half_ffn(
            h, m_ref[...], 0, win_ref, wout_ref, g_ref[...], b_ref[...], alpha)


def _ffn_in_call(srcs, mods, w_in, w_out, ln_g, ln_b, *, layer, alpha, n_ctx_tiles, t_all):
    bsz, _, d = srcs[0].shape
    d_ff = w_out.shape[-2]
    split = len(srcs) == 2
    tiles_per_sample = t_all // TM
    n = _tiles_per_step(bsz * tiles_per_sample, 3 * d * d_ff * 2, (len(srcs) + 1) * TM * d * 4)
    src_specs, mod_specs, operands = [], [], []
    for j in range(n):
        tile = _step_tile(n, tiles_per_sample, j)
        if split:
            def ctx_tile(p, tile=tile):
                bi, ti = tile(p)
                return bi, jnp.minimum(ti, n_ctx_tiles - 1), 0

            def lat_tile(p, tile=tile):
                bi, ti = tile(p)
                return bi, jnp.maximum(ti - n_ctx_tiles, 0), 0

            src_specs += [pl.BlockSpec((None, TM, d), ctx_tile), pl.BlockSpec((None, TM, d), lat_tile)]
        else:
            src_specs.append(pl.BlockSpec((None, TM, d), lambda p, tile=tile: (*tile(p), 0)))
        operands += list(srcs)
        mod_specs.append(_mod_spec(d, layer, n_ctx_tiles, tile_of=tile))
    out = pl.pallas_call(
        functools.partial(_ffn_in_kernel, alpha=alpha, n_ctx_tiles=n_ctx_tiles,
                          tiles_per_sample=tiles_per_sample, split=split, n=n),
        grid=(bsz * tiles_per_sample // n,),
        in_specs=src_specs + mod_specs + [
            _weight_spec((d, 2 * d_ff), layer, 0),
            _weight_spec((d_ff, d), layer, 0),
            _ln_spec(d, layer, 0), _ln_spec(d, layer, 0),
        ],
        out_specs=pl.BlockSpec((n * TM, d), lambda p: (p, 0)),
        out_shape=jax.ShapeDtypeStruct((bsz * t_all, d), F32),
        compiler_params=_cparams(1),
        name="ffn_in",
    )(*operands, *([mods] * n), w_in, w_out, ln_g, ln_b)
    return out.reshape(bsz, t_all, d)


def _post_kernel(*refs, alpha, mode, n):
    per_tile = 4 if mode == "product" else 3
    wmix_ref, g1_ref, b1_ref, win_ref, wout_ref, g2_ref, b2_ref, o_ref = refs[n * per_tile:]
    for j in range(n):
        h_ref, m_ref, *act_refs = refs[j * per_tile:(j + 1) * per_tile]
        if mode == "product":
            act = act_refs[0][...] * act_refs[1][...]
        else:
            act = act_refs[0][...].T.astype(BF16)
        m = m_ref[...]
        h = _layer_norm(alpha * h_ref[...] + m[5:6] * _dot(act, wmix_ref[...]), g1_ref[...], b1_ref[...])
        o_ref[j * TM:(j + 1) * TM, :] = _half_ffn(h, m, 6, win_ref, wout_ref, g2_ref[...], b2_ref[...], alpha)


def _post_call(h, mods, acts, w_mix, w_in, w_out, ln_g, ln_b, *, layer, idx, alpha, n_ctx_tiles,
               lat_only, mode):
    bsz, t_all, d = h.shape
    k = w_mix.shape[-2]
    d_ff = w_out.shape[-2]
    first_tile = n_ctx_tiles if lat_only else 0
    tiles_per_sample = t_all // TM - first_tile
    act_bytes = sum(a.dtype.itemsize for a in acts) * TM * k
    n = _tiles_per_step(bsz * tiles_per_sample, (3 * d * d_ff + k * d) * 2, 2 * TM * d * 4 + act_bytes)
    tile_specs, operands = [], []
    for j in range(n):
        tile = _step_tile(n, tiles_per_sample, j, first_tile)
        rows = lambda width, tile=tile: pl.BlockSpec((None, TM, width), lambda p: (*tile(p), 0))
        tile_specs += [rows(d), _mod_spec(d, layer, n_ctx_tiles, tile_of=tile)]
        if mode == "product":
            tile_specs += [rows(k) for _ in acts]
        else:
            tile_specs.append(pl.BlockSpec((None, None, k, TM), lambda p, tile=tile: (*tile(p), 0, 0)))
        operands += [h, mods, *acts]
    out = pl.pallas_call(
        functools.partial(_post_kernel, alpha=alpha, mode=mode, n=n),
        grid=(bsz * tiles_per_sample // n,),
        in_specs=tile_specs + [
            _weight_spec((k, d), idx),
            _ln_spec(d, layer, 1), _ln_spec(d, layer, 1),
            _weight_spec((d, 2 * d_ff), layer, 1),
            _weight_spec((d_ff, d), layer, 1),
            _ln_spec(d, layer, 2), _ln_spec(d, layer, 2),
        ],
        out_specs=pl.BlockSpec((n * TM, d), lambda p: (p, 0)),
        out_shape=jax.ShapeDtypeStruct((bsz * tiles_per_sample * TM, d), F32),
        compiler_params=_cparams(1),
        name="mixer_out_ffn_" + mode,
    )(*operands, w_mix, ln_g, ln_b, w_in, w_out, ln_g, ln_b)
    return out.reshape(bsz, tiles_per_sample * TM, d)


def _ret_proj_kernel(h_ref, m_ref, w_ref, cos_ref, sin_ref, q_ref, k_ref, v_ref, g_ref, *, heads):
    p = _dot(_modulated(h_ref, m_ref), w_ref[...])
    d_qk = heads * RET_DK
    d_v = heads * RET_DV
    cos = cos_ref[...]
    sin = sin_ref[...]

    def rope(x):
        parts = []
        for j in range(RET_DK // 128):
            sl = slice(j * 128, (j + 1) * 128)
            xs = x[:, sl]
            parts.append(xs * cos[:, sl] + pltpu.roll(xs, 64, 1) * sin[:, sl])
        return jnp.concatenate(parts, axis=1)

    for hh in range(heads):
        sl = slice(hh * RET_DK, (hh + 1) * RET_DK)
        q_ref[:, sl] = rope(p[:, sl]).astype(BF16)
        k_ref[:, sl] = rope(p[:, d_qk + hh * RET_DK:d_qk + (hh + 1) * RET_DK]
                            * (RET_DK ** -0.5)).astype(BF16)
    v_ref[...] = p[:, 2 * d_qk:2 * d_qk + d_v].astype(BF16)
    g_ref[...] = _silu(p[:, 2 * d_qk + d_v:]).astype(BF16)


def _ret_proj_call(h, mods, w_in, cos, sin, *, layer, idx, n_ctx_tiles):
    bsz, t_all, d = h.shape
    n = w_in.shape[-1]
    heads = n // (2 * RET_DK + 2 * RET_DV)
    d_qk, d_v = heads * RET_DK, heads * RET_DV
    shape = lambda w: jax.ShapeDtypeStruct((bsz, t_all, w), BF16)
    return pl.pallas_call(
        functools.partial(_ret_proj_kernel, heads=heads),
        grid=(bsz, t_all // TM),
        in_specs=[
            _tile_spec(d), _mod_spec(d, layer, n_ctx_tiles), _weight_spec((d, n), idx),
            pl.BlockSpec((TM, RET_DK), lambda bi, ti: (ti, 0)),
            pl.BlockSpec((TM, RET_DK), lambda bi, ti: (ti, 0)),
        ],
        out_specs=[_tile_spec(d_qk), _tile_spec(d_qk), _tile_spec(d_v), _tile_spec(d_v)],
        out_shape=[shape(d_qk), shape(d_qk), shape(d_v), shape(d_v)],
        compiler_params=_cparams(2),
        name="retention_proj",
    )(h, mods, w_in, cos, sin)


def _ret_core_kernel(q_ref, k_ref, v_ref, mask_ref, qdf_ref, qdb_ref, kdf_ref, kdb_ref,
                     cdf_ref, cdb_ref, o_ref, snap_s, sf_s, sb_s, *, n_chunks, n_ctx_chunks):
    c = RET_BLOCK
    nt = (((1,), (1,)), ((), ()))
    tn = (((0,), (0,)), ((), ()))

    sb_s[...] = jnp.zeros_like(sb_s)

    def back(i, carry):
        cb = jnp.where(i < n_ctx_chunks, n_ctx_chunks - 1 - i, n_chunks - 1 - (i - n_ctx_chunks))
        rb = pl.multiple_of(cb * c, c)
        s_prev = sb_s[...]
        snap_s[cb] = s_prev.astype(BF16)
        kd = (k_ref[pl.ds(rb, c), :] * kdb_ref[...]).astype(BF16)
        sb_s[...] = s_prev * cdb_ref[...] + lax.dot_general(
            kd, v_ref[pl.ds(rb, c), :], tn, preferred_element_type=F32)
        return carry

    lax.fori_loop(0, n_chunks, back, 0, unroll=RET_UNROLL)

    sf_s[...] = jnp.zeros_like(sf_s)

    def fwd(i, carry):
        r = pl.multiple_of(i * c, c)
        q = q_ref[pl.ds(r, c), :]
        k = k_ref[pl.ds(r, c), :]
        v = v_ref[pl.ds(r, c), :]
        s = lax.dot_general(q, k, nt, preferred_element_type=F32)
        att = (s * mask_ref[...]).astype(BF16)
        s_prev = sf_s[...]
        o = (_dot(att, v) + qdf_ref[...] * _dot(q, s_prev.astype(BF16))
             + qdb_ref[...] * _dot(q, snap_s[i]))
        kd = (k * kdf_ref[...]).astype(BF16)
        sf_s[...] = s_prev * cdf_ref[...] + lax.dot_general(kd, v, tn, preferred_element_type=F32)
        mu = jnp.mean(o, axis=-1, keepdims=True)
        oc = o - mu
        var = jnp.mean(oc * oc, axis=-1, keepdims=True)
        o_ref[pl.ds(r, c), :] = (oc * lax.rsqrt(var + LN_EPS)).astype(BF16)
        return carry

    lax.fori_loop(0, n_chunks, fwd, 0, unroll=RET_UNROLL)


def _ret_tables(heads):
    c = RET_BLOCK
    hs = np.arange(heads, dtype=np.float64)
    lgf = np.log1p(-np.exp2(-5.0 - hs))
    lgb = lgf[::-1]
    pos = np.arange(c, dtype=np.float64)
    diff = pos[:, None] - pos[None, :]
    mask = np.where(diff >= 0,
                    np.exp(lgf[:, None, None] * np.maximum(diff, 0.0)),
                    np.exp(lgb[:, None, None] * np.maximum(-diff, 0.0)))
    wide = lambda t, w: np.broadcast_to(t[:, :, None], (heads, c, w))
    qdf = wide(np.exp(lgf[:, None] * (pos + 1.0)), RET_DV)
    qdb = wide(np.exp(lgb[:, None] * (c - pos)), RET_DV)
    kdf = wide(np.exp(lgf[:, None] * (c - 1.0 - pos)), RET_DK)
    kdb = wide(np.exp(lgb[:, None] * pos), RET_DK)
    cdf = np.broadcast_to(np.exp(lgf * c)[:, None, None], (heads, 1, RET_DV))
    cdb = np.broadcast_to(np.exp(lgb * c)[:, None, None], (heads, 1, RET_DV))
    return tuple(np.ascontiguousarray(t, dtype=np.float32) for t in (mask, qdf, qdb, kdf, kdb, cdf, cdb))


def _ret_core_call(q, k, v, *, ctx_len):
    bsz, t_all, d_qk = q.shape
    heads = d_qk // RET_DK
    c = RET_BLOCK
    tables = _ret_tables(heads)
    per_head = lambda a: pl.BlockSpec((None,) + a.shape[1:], lambda hi, bi: (hi, 0, 0))
    seq = lambda w: pl.BlockSpec((None, t_all, w), lambda hi, bi: (bi, 0, hi))
    return pl.pallas_call(
        functools.partial(_ret_core_kernel, n_chunks=t_all // c, n_ctx_chunks=ctx_len // c),
        grid=(heads, bsz),
        in_specs=[seq(RET_DK), seq(RET_DK), seq(RET_DV), *[per_head(a) for a in tables]],
        out_specs=seq(RET_DV),
        out_shape=jax.ShapeDtypeStruct((bsz, t_all, heads * RET_DV), BF16),
        scratch_shapes=[
            pltpu.VMEM((t_all // c, RET_DK, RET_DV), BF16),
            pltpu.VMEM((RET_DK, RET_DV), F32),
            pltpu.VMEM((RET_DK, RET_DV), F32),
        ],
        compiler_params=_cparams(2),
        name="retention_core",
    )(q, k, v, *tables)


def _rope_tables(seq, ctx_len):
    half = RET_DK // 2
    freqs = ROPE_BASE ** (-np.arange(0, half, 2, dtype=np.float64) / half)
    pos = np.arange(seq)
    ar = (pos // GRID_W).astype(np.float64)[:, None] * freqs
    ac = (pos % GRID_W).astype(np.float64)[:, None] * freqs
    cos = np.concatenate([np.cos(ar), np.cos(ar), np.cos(ac), np.cos(ac)], axis=-1)
    sin = np.concatenate([-np.sin(ar), np.sin(ar), -np.sin(ac), np.sin(ac)], axis=-1)
    cos = np.concatenate([np.ones((ctx_len, RET_DK)), cos], axis=0)
    sin = np.concatenate([np.zeros((ctx_len, RET_DK)), sin], axis=0)
    return cos.astype(np.float32), sin.astype(np.float32)


def _na_proj_kernel(h_ref, m_ref, w_ref, k_ref, qt_ref, vt_ref, *, d_att):
    p = _dot(_modulated(h_ref, m_ref), w_ref[...])
    qt_ref[...] = (p[:, :d_att] * (NA_HEAD_DIM ** -0.5 * LOG2E)).T.astype(BF16)
    k_ref[...] = p[:, d_att:2 * d_att].astype(BF16)
    vt_ref[...] = p[:, 2 * d_att:].T.astype(BF16)


def _na_proj_call(h, mods, w_qkv, *, layer, idx, n_ctx_tiles):
    bsz, t_all, d = h.shape
    d_att = w_qkv.shape[-1] // 3
    n_tiles = t_all // TM
    t_spec = pl.BlockSpec((None, None, d_att, TM), lambda bi, ti: (bi, ti, 0, 0))
    t_shape = jax.ShapeDtypeStruct((bsz, n_tiles, d_att, TM), BF16)
    return pl.pallas_call(
        functools.partial(_na_proj_kernel, d_att=d_att),
        grid=(bsz, n_tiles),
        in_specs=[_tile_spec(d), _mod_spec(d, layer, n_ctx_tiles), _weight_spec((d, 3 * d_att), idx)],
        out_specs=[_tile_spec(d_att), t_spec, t_spec],
        out_shape=[jax.ShapeDtypeStruct((bsz, t_all, d_att), BF16), t_shape, t_shape],
        compiler_params=_cparams(2),
        name="na_proj",
    )(h, mods, w_qkv)


def _na_attn_kernel(k_ref, qt_ref, vt_ref, bias_ref, ot_ref, s0_s, s1_s, *, ctx_len, n_blocks):
    hd = NA_HEAD_DIM
    n_ctx_tiles = ctx_len // TM
    row = lax.broadcasted_iota(jnp.int32, (2 * hd, 2 * TM), 0)
    lane = lax.broadcasted_iota(jnp.int32, (2 * hd, 2 * TM), 1)
    own_head = (row < hd) == (lane < TM)
    k_ctx = k_ref[0:ctx_len, :]
    ones = jnp.ones((BF16_SUBLANES, TM), BF16)

    def pv(tile, p_tile):
        return _dot(jnp.concatenate([vt_ref[tile], ones], axis=0), p_tile.astype(BF16))

    def queries(q_tile):
        qt2 = qt_ref[q_tile]
        qt = jnp.concatenate([qt2, qt2], axis=1)
        return jnp.where(own_head, qt, jnp.zeros_like(qt))

    def probs(s, m):
        return jnp.exp2((s - m).astype(BF16))

    def write_out(q_tile, acc):
        o0 = acc[0:hd, 0:TM] * (1.0 / acc[2 * hd:2 * hd + 1, 0:TM])
        o1 = acc[hd:2 * hd, TM:2 * TM] * (1.0 / acc[2 * hd:2 * hd + 1, TM:2 * TM])
        ot_ref[q_tile] = jnp.concatenate([o0, o1], axis=0)

    for t in range(n_ctx_tiles):
        s_ctx = _dot(k_ctx, queries(t))
        p_ctx = probs(s_ctx, jnp.max(s_ctx, axis=0, keepdims=True))
        acc = pv(0, p_ctx[0:TM])
        for j in range(1, n_ctx_tiles):
            acc = acc + pv(j, p_ctx[j * TM:(j + 1) * TM])
        write_out(t, acc)

    def key_tile(blk):
        return n_ctx_tiles + jnp.clip(blk - 1, 0, n_blocks - NA_WIN_TILES)

    def scores(blk, s_s):
        qt = queries(n_ctx_tiles + blk)
        variant = jnp.where(blk == 0, 0, jnp.where(blk == n_blocks - 1, 2, 1))
        k_off = pl.multiple_of(key_tile(blk) * TM, TM)
        s_ctx = _dot(k_ctx, qt)
        s_loc = _dot(k_ref[pl.ds(k_off, NA_WIN_TILES * TM), :], qt) + bias_ref[variant]
        s_s[0:ctx_len, :] = s_ctx
        s_s[ctx_len:ctx_len + NA_WIN_TILES * TM, :] = s_loc
        return jnp.maximum(jnp.max(s_ctx, axis=0, keepdims=True), jnp.max(s_loc, axis=0, keepdims=True))

    def output(blk, s_s, m):
        k_tile = key_tile(blk)
        acc = None
        for j in range(n_ctx_tiles + NA_WIN_TILES):
            p = probs(s_s[j * TM:(j + 1) * TM, :], m)
            part = pv(j if j < n_ctx_tiles else k_tile + (j - n_ctx_tiles), p)
            acc = part if acc is None else acc + part
        write_out(n_ctx_tiles + blk, acc)

    def step(j, m_even):
        m_odd = scores(2 * j + 1, s1_s)
        output(2 * j, s0_s, m_even)
        m_even = scores(2 * j + 2, s0_s)
        output(2 * j + 1, s1_s, m_odd)
        return m_even

    m_even = lax.fori_loop(0, n_blocks // 2 - 1, step, scores(0, s0_s))
    m_odd = scores(n_blocks - 1, s1_s)
    output(n_blocks - 2, s0_s, m_even)
    output(n_blocks - 1, s1_s, m_odd)


def _na_bias_tables(rpb, grid_rows):
    heads = rpb.shape[0]
    w = GRID_W
    col = jnp.arange(w)
    start = jnp.clip(col - NA_WIN_C // 2, 0, w - NA_WIN_C)
    ok = (col[:, None] >= start[None, :]) & (col[:, None] < start[None, :] + NA_WIN_C)
    rel_c = jnp.clip(col[:, None] - col[None, :] + NA_WIN_C - 1, 0, 2 * NA_WIN_C - 2)
    colb = jnp.where(ok[None, None], rpb[:, :, rel_c].astype(F32) * LOG2E, NEG_INF)
    masked = jnp.full((heads, w, w), NEG_INF, F32)
    key_rows = NA_WIN_TILES * TM // w

    def build(q_row0, k_row0):
        out = []
        for kj in range(key_rows):
            blocks = []
            for qi in range(NA_ROWS_PER_BLOCK):
                r, kr = q_row0 + qi, k_row0 + kj
                rs = min(max(r - NA_WIN_R // 2, 0), grid_rows - NA_WIN_R)
                inside = rs <= kr < rs + NA_WIN_R
                blocks.append(colb[:, kr - r + NA_WIN_R - 1] if inside else masked)
            out.append(jnp.concatenate(blocks, axis=2))
        return jnp.concatenate(out, axis=1)

    rpb_blk = NA_ROWS_PER_BLOCK
    variants = jnp.stack([build(0, 0), build(rpb_blk, 0),
                          build(grid_rows - rpb_blk, grid_rows - key_rows)], axis=1)
    v = variants.reshape(heads // 2, 2, 3, key_rows * w, TM)
    return jnp.concatenate([v[:, 0], v[:, 1]], axis=-1)


def _na_attn_call(k, qt, vt, rpb, *, ctx_len):
    bsz, t_all, d_att = k.shape
    n_tiles = t_all // TM
    pairs = d_att // (2 * NA_HEAD_DIM)
    n_blocks = (t_all - ctx_len) // TM
    assert n_blocks >= NA_WIN_TILES and n_blocks % 2 == 0
    bias = _na_bias_tables(rpb, (t_all - ctx_len) // GRID_W)
    t_spec = pl.BlockSpec((None, n_tiles, 2 * NA_HEAD_DIM, TM), lambda pi, bi: (bi, 0, pi, 0))
    return pl.pallas_call(
        functools.partial(_na_attn_kernel, ctx_len=ctx_len, n_blocks=n_blocks),
        grid=(pairs, bsz),
        in_specs=[
            pl.BlockSpec((None, t_all, 2 * NA_HEAD_DIM), lambda pi, bi: (bi, 0, pi)),
            t_spec, t_spec,
            pl.BlockSpec((None,) + bias.shape[1:], lambda pi, bi: (pi, 0, 0, 0)),
        ],
        out_specs=t_spec,
        out_shape=jax.ShapeDtypeStruct((bsz, n_tiles, d_att, TM), F32),
        scratch_shapes=[pltpu.VMEM((ctx_len + NA_WIN_TILES * TM, 2 * TM), F32)] * 2,
        compiler_params=_cparams(2),
        name="na_attention",
    )(k, qt, vt, bias)


def _lru_proj_kernel(h_ref, m_ref, w_ref, gate_ref, x_ref, *, width):
    p = _dot(_modulated(h_ref, m_ref), w_ref[...])
    gate_ref[...] = _gelu_tanh(p[:, :width]).astype(BF16)
    x_ref[...] = p[:, width:]


def _lru_proj_call(h, mods, w_in, *, layer, idx, n_ctx_tiles):
    bsz, t_all, d = h.shape
    width = w_in.shape[-1] // 2
    return pl.pallas_call(
        functools.partial(_lru_proj_kernel, width=width),
        grid=(bsz, t_all // TM),
        in_specs=[_tile_spec(d), _mod_spec(d, layer, n_ctx_tiles), _weight_spec((d, 2 * width), idx)],
        out_specs=[_tile_spec(width), _tile_spec(width)],
        out_shape=[jax.ShapeDtypeStruct((bsz, t_all, width), BF16),
                   jax.ShapeDtypeStruct((bsz, t_all, width), F32)],
        compiler_params=_cparams(2),
        name="rglru_proj",
    )(h, mods, w_in)


def _lru_core_kernel(x_ref, cw_ref, cb_ref, wa_ref, ba_ref, wx_ref, bx_ref, lam_ref, o_ref,
                     xp_s, a_s, u_s, *, t_all, ctx_len):
    pad = SUBLANES
    ch = LRU_CHUNK
    bw = LRU_BLOCK_W
    n_chunks = t_all // ch
    n_ctx_chunks = ctx_len // ch

    zeros = jnp.zeros((pad, bw), F32)
    xp_s[0:pad, :] = zeros
    xp_s[pad + ctx_len:2 * pad + ctx_len, :] = zeros
    xp_s[2 * pad + t_all:3 * pad + t_all, :] = zeros

    def padded_row(i):
        return pl.multiple_of(i * ch + jnp.where(i >= n_ctx_chunks, 2 * pad, pad), SUBLANES)

    def copy(i, carry):
        r = pl.multiple_of(i * ch, ch)
        xp_s[pl.ds(padded_row(i), ch), :] = x_ref[pl.ds(r, ch), :]
        return carry

    lax.fori_loop(0, n_chunks, copy, 0)

    cw = cw_ref[...]
    cb = cb_ref[...]
    neg_lam = -lam_ref[...]
    softplus = jnp.maximum(neg_lam, 0.0) + jnp.log1p(jnp.exp(-jnp.abs(neg_lam)))
    decay_rate = -LRU_C * softplus

    def gates(i, carry):
        r = pl.multiple_of(i * ch, ch)
        lo = pl.multiple_of(padded_row(i) - pad, SUBLANES)
        xe = xp_s[pl.ds(lo, ch + 2 * pad), :]
        mid = slice(pad, pad + ch)
        x_m2 = pltpu.roll(xe, 2, 0)[mid]
        x_m1 = pltpu.roll(xe, 1, 0)[mid]
        x_p1 = pltpu.roll(xe, ch + 2 * pad - 1, 0)[mid]
        x = cw[0:1] * x_m2 + cw[1:2] * x_m1 + cw[2:3] * xe[mid] + cw[3:4] * x_p1 + cb
        xb = x.astype(BF16)
        for d in range(2):
            rg = _sigmoid(_dot(xb, wa_ref[d]) + ba_ref[d:d + 1])
            ig = _sigmoid(_dot(xb, wx_ref[d]) + bx_ref[d:d + 1])
            log_a = rg * decay_rate[d:d + 1]
            a_s[d, pl.ds(r, ch), :] = jnp.exp(log_a)
            th = jnp.tanh(log_a)
            m2 = -2.0 * th / (1.0 - th)
            mult = m2 * lax.rsqrt(jnp.maximum(m2, 1e-37))
            u_s[d, pl.ds(r, ch), :] = mult * (ig * x)
        return carry

    lax.fori_loop(0, n_chunks, gates, 0)

    sub = lax.broadcasted_iota(jnp.int32, (SUBLANES, bw), 0)
    rows_per_iter = SUBLANES * SCAN_GROUPS
    n_blocks = t_all // rows_per_iter
    n_ctx_blocks = ctx_len // rows_per_iter

    def scan8(a, u, reverse):
        for s in (1, 2, 4):
            if reverse:
                valid = sub < SUBLANES - s
                shift = SUBLANES - s
            else:
                valid = sub >= s
                shift = s
            u = u + a * jnp.where(valid, pltpu.roll(u, shift, 0), 0.0)
            a = a * jnp.where(valid, pltpu.roll(a, shift, 0), 1.0)
        return a, u

    def rec(i, carry):
        hf, hb = carry
        rf = pl.multiple_of(i * rows_per_iter, rows_per_iter)
        bb = jnp.where(i < n_ctx_blocks, n_ctx_blocks - 1 - i, n_blocks - 1 - (i - n_ctx_blocks))
        rb = pl.multiple_of(bb * rows_per_iter, rows_per_iter)
        af, uf = a_s[0, pl.ds(rf, rows_per_iter), :], u_s[0, pl.ds(rf, rows_per_iter), :]
        ab, ub = a_s[1, pl.ds(rb, rows_per_iter), :], u_s[1, pl.ds(rb, rows_per_iter), :]
        outs_f, outs_b = [], []
        for j in range(SCAN_GROUPS):
            g = slice(j * SUBLANES, (j + 1) * SUBLANES)
            a, u = scan8(af[g], uf[g], False)
            h = u + a * hf
            outs_f.append(h)
            hf = h[SUBLANES - 1:SUBLANES]
            g = slice((SCAN_GROUPS - 1 - j) * SUBLANES, (SCAN_GROUPS - j) * SUBLANES)
            a, u = scan8(ab[g], ub[g], True)
            h = u + a * hb
            outs_b.append(h)
            hb = h[0:1]
        u_s[0, pl.ds(rf, rows_per_iter), :] = jnp.concatenate(outs_f, axis=0)
        u_s[1, pl.ds(rb, rows_per_iter), :] = jnp.concatenate(outs_b[::-1], axis=0)
        return hf, hb

    zero = jnp.zeros((1, bw), F32)
    lax.fori_loop(0, n_blocks, rec, (zero, zero))

    def emit(i, carry):
        r = pl.multiple_of(i * ch, ch)
        o_ref[pl.ds(r, ch), :] = (u_s[0, pl.ds(r, ch), :] + u_s[1, pl.ds(r, ch), :]).astype(BF16)
        return carry

    lax.fori_loop(0, n_chunks, emit, 0)


def _lru_core_call(x, conv_w, conv_b, w_a, b_a, w_x, b_x, lam, *, idx, ctx_len):
    bsz, t_all, width = x.shape
    bw = LRU_BLOCK_W
    blocks = width // bw
    col = lambda rows: pl.BlockSpec((None, rows, bw), lambda ki, bi: (idx, 0, ki))
    wspec = pl.BlockSpec((None, 2, None, bw, bw), lambda ki, bi: (idx, 0, ki, 0, 0))
    seq = pl.BlockSpec((None, t_all, bw), lambda ki, bi: (bi, 0, ki))
    return pl.pallas_call(
        functools.partial(_lru_core_kernel, t_all=t_all, ctx_len=ctx_len),
        grid=(blocks, bsz),
        in_specs=[seq, col(LRU_CONV_W), col(1), wspec, col(2), wspec, col(2), col(2)],
        out_specs=seq,
        out_shape=jax.ShapeDtypeStruct((bsz, t_all, width), BF16),
        scratch_shapes=[
            pltpu.VMEM((t_all + 3 * SUBLANES, bw), F32),
            pltpu.VMEM((2, t_all, bw), F32),
            pltpu.VMEM((2, t_all, bw), F32),
        ],
        compiler_params=_cparams(2),
        name="rglru_core",
    )(x, conv_w, conv_b, w_a, b_a, w_x, b_x, lam)


def kernel(x, c, ctx, c_ctx, ada_w, ada_b, ln_g, ln_b, ffn_w_in, ffn_w_out, ret_w_in, ret_w_out,
           na_w_qkv, na_rpb, na_w_out, lru_w_in, lru_conv_w, lru_conv_b, lru_w_a, lru_b_a,
           lru_w_x, lru_b_x, lru_lam, lru_w_out):
    bsz, seq, d = x.shape
    ctx_len = ctx.shape[1]
    t_all = ctx_len + seq
    depth = ada_w.shape[0]
    alpha = (2 * depth) ** 0.25
    assert ctx_len % TM == 0 and seq % TM == 0 and ctx_len % LRU_CHUNK == 0
    assert ctx_len % RET_BLOCK == 0 and seq % RET_BLOCK == 0
    assert NA_ROWS_PER_BLOCK * GRID_W == TM
    assert (NA_WIN_R + NA_ROWS_PER_BLOCK) * GRID_W == NA_WIN_TILES * TM
    n_ctx_tiles = ctx_len // TM

    s = jnp.concatenate([jnp.broadcast_to(c_ctx[None, :], (bsz, d)), c], axis=0)
    mods = _mods_call(s, ada_w, ada_b).reshape(depth, 2, bsz, N_MOD, d)
    cos, sin = _rope_tables(seq, ctx_len)
    bf = lambda w: w.astype(BF16)
    ffn_w_in, ffn_w_out = bf(ffn_w_in), bf(ffn_w_out)
    ret_w_in, ret_w_out = bf(ret_w_in), bf(ret_w_out)
    na_w_qkv, na_w_out = bf(na_w_qkv), bf(na_w_out)
    lru_w_in, lru_w_out, lru_w_a, lru_w_x = bf(lru_w_in), bf(lru_w_out), bf(lru_w_a), bf(lru_w_x)
    ln_g = ln_g.reshape(depth, 3, 1, d)
    ln_b = ln_b.reshape(depth, 3, 1, d)
    lru_conv_b = lru_conv_b[:, None, :]

    h = None
    for layer in range(depth):
        last = layer == depth - 1
        kind, idx = layer % 3, layer // 3
        lay = dict(layer=layer, idx=idx, n_ctx_tiles=n_ctx_tiles)
        h = _ffn_in_call([ctx, x] if layer == 0 else [h], mods, ffn_w_in, ffn_w_out, ln_g, ln_b,
                         layer=layer, alpha=alpha, n_ctx_tiles=n_ctx_tiles, t_all=t_all)
        if kind == 0:
            q, k, v, sg = _ret_proj_call(h, mods, ret_w_in, cos, sin, **lay)
            acts, w_mix, mode = [sg, _ret_core_call(q, k, v, ctx_len=ctx_len)], ret_w_out, "product"
        elif kind == 1:
            k, qt, vt = _na_proj_call(h, mods, na_w_qkv, **lay)
            acts, w_mix, mode = [_na_attn_call(k, qt, vt, na_rpb[idx], ctx_len=ctx_len)], na_w_out, "transposed"
        else:
            gate, xr = _lru_proj_call(h, mods, lru_w_in, **lay)
            hs = _lru_core_call(xr, lru_conv_w, lru_conv_b, lru_w_a, lru_b_a, lru_w_x, lru_b_x,
                                lru_lam, idx=idx, ctx_len=ctx_len)
            acts, w_mix, mode = [gate, hs], lru_w_out, "product"
        h = _post_call(h, mods, acts, w_mix, ffn_w_in, ffn_w_out, ln_g, ln_b, alpha=alpha,
                       lat_only=last, mode=mode, **lay)
    return h
```

```python
import functools

import jax
import jax.numpy as jnp
import numpy as np
from jax import lax
from jax.experimental import pallas as pl
from jax.experimental.pallas import tpu as pltpu

F32 = jnp.float32
BF16 = jnp.bfloat16

GRID_W = 64
N_MOD = 9
FFN_RES = 0.5
RET_DK = 256
RET_DV = 512
RET_BLOCK = 256
RET_UNROLL = True
ROPE_BASE = 10000.0
NA_HEAD_DIM = 64
NA_WIN_R = 8
NA_WIN_C = 16
NA_ROWS_PER_BLOCK = 4
NA_WIN_TILES = 3
NEG_INF = -1e30
LOG2E = 1.4426950408889634
LRU_BLOCK_W = 256
LRU_CONV_W = 4
LRU_C = 8.0
LN_EPS = 1e-5

TM = 256
MAX_TILES_PER_STEP = 4
LRU_CHUNK = 256
SUBLANES = 8
BF16_SUBLANES = 16
SCAN_GROUPS = 8
VMEM_LIMIT = 56 * 1024 * 1024
FFN_TEMPORARIES = 4 * 1024 * 1024
POST_TEMPORARIES = 8 * 1024 * 1024
PROJ_TEMPORARIES = 8 * 1024 * 1024


def _cparams(n_axes):
    return pltpu.CompilerParams(
        dimension_semantics=("arbitrary",) * n_axes, vmem_limit_bytes=VMEM_LIMIT)


def _weight_spec(tail, *lead):
    zeros = (0,) * len(tail)
    return pl.BlockSpec((None,) * len(lead) + tuple(tail), lambda *_: tuple(lead) + zeros,
                        pipeline_mode=pl.Buffered(1))


def _mod_spec(d, layer, n_ctx_tiles, off=0, tile_of=None):
    def index(*idx):
        bi, ti = tile_of(*idx) if tile_of else idx
        return layer, jnp.where(ti + off >= n_ctx_tiles, 1, 0), bi, 0, 0

    return pl.BlockSpec((None, None, None, N_MOD, d), index)


def _ln_spec(d, layer, j):
    return pl.BlockSpec((None, None, 1, d), lambda *_: (layer, j, 0, 0))


def _silu(x):
    return x * (1.0 / (1.0 + jnp.exp(-x)))


def _gelu_tanh(x):
    return 0.5 * x * (1.0 + jnp.tanh(0.7978845608028654 * (x + 0.044715 * (x * x * x))))


def _layer_norm(z, g, b):
    mu = jnp.mean(z, axis=-1, keepdims=True)
    zc = z - mu
    var = jnp.mean(zc * zc, axis=-1, keepdims=True)
    return zc * lax.rsqrt(var + LN_EPS) * g + b


def _dot(a, b):
    return jnp.dot(a, b, preferred_element_type=F32)


def _half_ffn(h, m, j0, win_ref, wout_ref, g, b, alpha):
    d_ff = wout_ref.shape[0]
    u = (h * (1.0 + m[j0 + 1:j0 + 2]) + m[j0:j0 + 1]).astype(BF16)
    gu = _dot(u, win_ref[...])
    a = (_silu(gu[:, :d_ff]) * gu[:, d_ff:]).astype(BF16)
    y = _dot(a, wout_ref[...])
    return _layer_norm(alpha * h + m[j0 + 2:j0 + 3] * (FFN_RES * y), g, b)


def _modulated(h_ref, m_ref):
    m = m_ref[...]
    return (h_ref[...] * (1.0 + m[4:5]) + m[3:4]).astype(BF16)


def _mods_kernel(s_ref, w_ref, b_ref, o_ref):
    s = _silu(s_ref[...]).astype(BF16)
    o_ref[0] = _dot(s, w_ref[0].astype(BF16)) + b_ref[0]


def _mods_call(s, ada_w, ada_b):
    depth, d, n = ada_w.shape
    tn = 1024
    rows = s.shape[0]
    return pl.pallas_call(
        _mods_kernel,
        grid=(depth, n // tn),
        in_specs=[
            pl.BlockSpec((rows, d), lambda l, j: (0, 0)),
            pl.BlockSpec((1, d, tn), lambda l, j: (l, 0, j)),
            pl.BlockSpec((1, 1, tn), lambda l, j: (l, 0, j)),
        ],
        out_specs=pl.BlockSpec((1, rows, tn), lambda l, j: (l, 0, j)),
        out_shape=jax.ShapeDtypeStruct((depth, rows, n), F32),
        compiler_params=_cparams(2),
        name="adaln_mods",
    )(s, ada_w, ada_b.reshape(depth, 1, n))


def _tiles_per_step(n_tiles, resident_bytes, tile_io_bytes, temporaries):
    n = MAX_TILES_PER_STEP
    while n > 1 and (n_tiles % n or
                     resident_bytes + 2 * n * tile_io_bytes + temporaries > VMEM_LIMIT):
        n //= 2
    return n


def _step_tile(n, tiles_per_sample, j, first_tile=0):
    def tile(p):
        g = n * p + j
        return g // tiles_per_sample, g % tiles_per_sample + first_tile

    return tile


def _ffn_in_kernel(*refs, alpha, n_ctx_tiles, tiles_per_sample, split, n):
    n_src = 2 * n if split else n
    srcs, m_refs = refs[:n_src], refs[n_src:n_src + n]
    win_ref, wout_ref, g_ref, b_ref, o_ref = refs[n_src + n:]
    for j, m_ref in enumerate(m_refs):
        if split:
            tile = (n * pl.program_id(0) + j) % tiles_per_sample
            h = jnp.where(tile < n_ctx_tiles, srcs[2 * j][...], srcs[2 * j + 1][...])
        else:
            h = srcs[j][...]
        o_ref[j * TM:(j + 1) * TM, :] = _half_ffn(
            h, m_ref[...], 0, win_ref, wout_ref, g_ref[...], b_ref[...], alpha)


def _ffn_in_call(srcs, mods, w_in, w_out, ln_g, ln_b, *, layer, alpha, n_ctx_tiles, t_all):
    bsz, _, d = srcs[0].shape
    d_ff = w_out.shape[-2]
    split = len(srcs) == 2
    tiles_per_sample = t_all // TM
    n = _tiles_per_step(bsz * tiles_per_sample, 3 * d * d_ff * 2, (len(srcs) + 1) * TM * d * 4,
                        FFN_TEMPORARIES)
    src_specs, mod_specs, operands = [], [], []
    for j in range(n):
        tile = _step_tile(n, tiles_per_sample, j)
        if split:
            def ctx_tile(p, tile=tile):
                bi, ti = tile(p)
                return bi, jnp.minimum(ti, n_ctx_tiles - 1), 0

            def lat_tile(p, tile=tile):
                bi, ti = tile(p)
                return bi, jnp.maximum(ti - n_ctx_tiles, 0), 0

            src_specs += [pl.BlockSpec((None, TM, d), ctx_tile), pl.BlockSpec((None, TM, d), lat_tile)]
        else:
            src_specs.append(pl.BlockSpec((None, TM, d), lambda p, tile=tile: (*tile(p), 0)))
        operands += list(srcs)
        mod_specs.append(_mod_spec(d, layer, n_ctx_tiles, tile_of=tile))
    out = pl.pallas_call(
        functools.partial(_ffn_in_kernel, alpha=alpha, n_ctx_tiles=n_ctx_tiles,
                          tiles_per_sample=tiles_per_sample, split=split, n=n),
        grid=(bsz * tiles_per_sample // n,),
        in_specs=src_specs + mod_specs + [
            _weight_spec((d, 2 * d_ff), layer, 0),
            _weight_spec((d_ff, d), layer, 0),
            _ln_spec(d, layer, 0), _ln_spec(d, layer, 0),
        ],
        out_specs=pl.BlockSpec((n * TM, d), lambda p: (p, 0)),
        out_shape=jax.ShapeDtypeStruct((bsz * t_all, d), F32),
        compiler_params=_cparams(1),
        name="ffn_in",
    )(*operands, *([mods] * n), w_in, w_out, ln_g, ln_b)
    return out.reshape(bsz, t_all, d)


def _post_kernel(*refs, alpha, transposed, n):
    per_tile = 3
    wmix_ref, g1_ref, b1_ref, win_ref, wout_ref, g2_ref, b2_ref, o_ref = refs[n * per_tile:]
    for j in range(n):
        h_ref, m_ref, act_ref = refs[j * per_tile:(j + 1) * per_tile]
        if transposed:
            act = act_ref[...].T.astype(BF16)
        else:
            act = act_ref[...]
        m = m_ref[...]
        h = _layer_norm(alpha * h_ref[...] + m[5:6] * _dot(act, wmix_ref[...]), g1_ref[...], b1_ref[...])
        o_ref[j * TM:(j + 1) * TM, :] = _half_ffn(h, m, 6, win_ref, wout_ref, g2_ref[...], b2_ref[...], alpha)


def _post_call(h, mods, act, w_mix, w_in, w_out, ln_g, ln_b, *, layer, idx, alpha, n_ctx_tiles,
               lat_only):
    bsz, t_all, d = h.shape
    k = w_mix.shape[-2]
    d_ff = w_out.shape[-2]
    transposed = act.ndim == 4
    first_tile = n_ctx_tiles if lat_only else 0
    tiles_per_sample = t_all // TM - first_tile
    n = _tiles_per_step(bsz * tiles_per_sample, (3 * d * d_ff + k * d) * 2,
                        2 * TM * d * 4 + TM * k * act.dtype.itemsize, POST_TEMPORARIES)
    tile_specs, operands = [], []
    for j in range(n):
        tile = _step_tile(n, tiles_per_sample, j, first_tile)
        rows = lambda width, tile=tile: pl.BlockSpec((None, TM, width), lambda p: (*tile(p), 0))
        tile_specs += [rows(d), _mod_spec(d, layer, n_ctx_tiles, tile_of=tile)]
        if transposed:
            tile_specs.append(pl.BlockSpec((None, None, k, TM), lambda p, tile=tile: (*tile(p), 0, 0)))
        else:
            tile_specs.append(rows(k))
        operands += [h, mods, act]
    out = pl.pallas_call(
        functools.partial(_post_kernel, alpha=alpha, transposed=transposed, n=n),
        grid=(bsz * tiles_per_sample // n,),
        in_specs=tile_specs + [
            _weight_spec((k, d), idx),
            _ln_spec(d, layer, 1), _ln_spec(d, layer, 1),
            _weight_spec((d, 2 * d_ff), layer, 1),
            _weight_spec((d_ff, d), layer, 1),
            _ln_spec(d, layer, 2), _ln_spec(d, layer, 2),
        ],
        out_specs=pl.BlockSpec((n * TM, d), lambda p: (p, 0)),
        out_shape=jax.ShapeDtypeStruct((bsz * tiles_per_sample * TM, d), F32),
        compiler_params=_cparams(1),
        name="mixer_out_ffn",
    )(*operands, w_mix, ln_g, ln_b, w_in, w_out, ln_g, ln_b)
    return out.reshape(bsz, tiles_per_sample * TM, d)


def _proj_kernel(*refs, epilogue, n, n_tables):
    per_tile = 2 + n_tables
    w_ref = refs[n * per_tile]
    outs = refs[n * per_tile + 1:]
    for j in range(n):
        h_ref, m_ref, *tables = refs[j * per_tile:(j + 1) * per_tile]
        epilogue(j, _dot(_modulated(h_ref, m_ref), w_ref[...]), tables, outs)


def _proj_call(name, epilogue, h, mods, w, tables, outs, *, layer, idx, n_ctx_tiles):
    bsz, t_all, d = h.shape
    tiles_per_sample = t_all // TM
    n_tiles = bsz * tiles_per_sample
    io_bytes = TM * (d * 4 + sum(t.shape[-1] * 4 for t in tables)
                     + sum(width * jnp.dtype(dt).itemsize for _, width, dt in outs))
    n = _tiles_per_step(n_tiles, w.shape[-2] * w.shape[-1] * 2, io_bytes, PROJ_TEMPORARIES)
    tile_specs, operands = [], []
    for j in range(n):
        tile = _step_tile(n, tiles_per_sample, j)
        tile_specs += [pl.BlockSpec((None, TM, d), lambda p, tile=tile: (*tile(p), 0)),
                       _mod_spec(d, layer, n_ctx_tiles, tile_of=tile)]
        tile_specs += [pl.BlockSpec((TM, t.shape[-1]), lambda p, tile=tile: (tile(p)[1], 0))
                       for t in tables]
        operands += [h, mods, *tables]
    out_specs, out_shapes, final_shapes = [], [], []
    for layout, width, dt in outs:
        if layout == "rows":
            out_specs.append(pl.BlockSpec((n * TM, width), lambda p: (p, 0)))
            out_shapes.append(jax.ShapeDtypeStruct((n_tiles * TM, width), dt))
            final_shapes.append((bsz, t_all, width))
        else:
            out_specs.append(pl.BlockSpec((n, width, TM), lambda p: (p, 0, 0)))
            out_shapes.append(jax.ShapeDtypeStruct((n_tiles, width, TM), dt))
            final_shapes.append((bsz, tiles_per_sample, width, TM))
    res = pl.pallas_call(
        functools.partial(_proj_kernel, epilogue=epilogue, n=n, n_tables=len(tables)),
        grid=(n_tiles // n,),
        in_specs=tile_specs + [_weight_spec(w.shape[-2:], idx)],
        out_specs=out_specs,
        out_shape=out_shapes,
        compiler_params=_cparams(1),
        name=name,
    )(*operands, w)
    return [r.reshape(s) for r, s in zip(res, final_shapes)]


def _ret_proj_epilogue(j, p, tables, outs, *, heads):
    cos_ref, sin_ref = tables
    q_ref, k_ref, v_ref, g_ref = outs
    rows = slice(j * TM, (j + 1) * TM)
    d_qk = heads * RET_DK
    d_v = heads * RET_DV
    cos = cos_ref[...]
    sin = sin_ref[...]

    def rope(x):
        parts = []
        for c in range(RET_DK // 128):
            sl = slice(c * 128, (c + 1) * 128)
            xs = x[:, sl]
            parts.append(xs * cos[:, sl] + pltpu.roll(xs, 64, 1) * sin[:, sl])
        return jnp.concatenate(parts, axis=1)

    for hh in range(heads):
        sl = slice(hh * RET_DK, (hh + 1) * RET_DK)
        q_ref[rows, sl] = rope(p[:, sl]).astype(BF16)
        k_ref[rows, sl] = rope(p[:, d_qk + hh * RET_DK:d_qk + (hh + 1) * RET_DK]
                               * (RET_DK ** -0.5)).astype(BF16)
    v_ref[rows, :] = p[:, 2 * d_qk:2 * d_qk + d_v].astype(BF16)
    g_ref[rows, :] = _silu(p[:, 2 * d_qk + d_v:]).astype(BF16)


def _ret_proj_call(h, mods, w_in, cos, sin, **layer_args):
    heads = w_in.shape[-1] // (2 * RET_DK + 2 * RET_DV)
    d_qk, d_v = heads * RET_DK, heads * RET_DV
    return _proj_call("retention_proj", functools.partial(_ret_proj_epilogue, heads=heads),
                      h, mods, w_in, [cos, sin],
                      [("rows", d_qk, BF16), ("rows", d_qk, BF16), ("rows", d_v, BF16), ("rows", d_v, BF16)],
                      **layer_args)


def _ret_core_kernel(q_ref, k_ref, v_ref, g_ref, mask_ref, qdf_ref, qdb_ref, kdf_ref, kdb_ref,
                     cdf_ref, cdb_ref, o_ref, snap_s, sf_s, sb_s, *, n_chunks, n_ctx_chunks):
    c = RET_BLOCK
    nt = (((1,), (1,)), ((), ()))
    tn = (((0,), (0,)), ((), ()))

    sb_s[...] = jnp.zeros_like(sb_s)

    def back(i, carry):
        cb = jnp.where(i < n_ctx_chunks, n_ctx_chunks - 1 - i, n_chunks - 1 - (i - n_ctx_chunks))
        rb = pl.multiple_of(cb * c, c)
        s_prev = sb_s[...]
        snap_s[cb] = s_prev.astype(BF16)
        kd = (k_ref[pl.ds(rb, c), :] * kdb_ref[...]).astype(BF16)
        sb_s[...] = s_prev * cdb_ref[...] + lax.dot_general(
            kd, v_ref[pl.ds(rb, c), :], tn, preferred_element_type=F32)
        return carry

    lax.fori_loop(0, n_chunks, back, 0, unroll=RET_UNROLL)

    sf_s[...] = jnp.zeros_like(sf_s)

    def fwd(i, carry):
        r = pl.multiple_of(i * c, c)
        q = q_ref[pl.ds(r, c), :]
        k = k_ref[pl.ds(r, c), :]
        v = v_ref[pl.ds(r, c), :]
        s = lax.dot_general(q, k, nt, preferred_element_type=F32)
        att = (s * mask_ref[...]).astype(BF16)
        s_prev = sf_s[...]
        o = (_dot(att, v) + qdf_ref[...] * _dot(q, s_prev.astype(BF16))
             + qdb_ref[...] * _dot(q, snap_s[i]))
        kd = (k * kdf_ref[...]).astype(BF16)
        sf_s[...] = s_prev * cdf_ref[...] + lax.dot_general(kd, v, tn, preferred_element_type=F32)
        mu = jnp.mean(o, axis=-1, keepdims=True)
        oc = o - mu
        var = jnp.mean(oc * oc, axis=-1, keepdims=True)
        o_ref[pl.ds(r, c), :] = (oc * lax.rsqrt(var + LN_EPS) * g_ref[pl.ds(r, c), :]).astype(BF16)
        return carry

    lax.fori_loop(0, n_chunks, fwd, 0, unroll=RET_UNROLL)


def _ret_tables(heads):
    c = RET_BLOCK
    hs = np.arange(heads, dtype=np.float64)
    lgf = np.log1p(-np.exp2(-5.0 - hs))
    lgb = lgf[::-1]
    pos = np.arange(c, dtype=np.float64)
    diff = pos[:, None] - pos[None, :]
    mask = np.where(diff >= 0,
                    np.exp(lgf[:, None, None] * np.maximum(diff, 0.0)),
                    np.exp(lgb[:, None, None] * np.maximum(-diff, 0.0)))
    wide = lambda t, w: np.broadcast_to(t[:, :, None], (heads, c, w))
    qdf = wide(np.exp(lgf[:, None] * (pos + 1.0)), RET_DV)
    qdb = wide(np.exp(lgb[:, None] * (c - pos)), RET_DV)
    kdf = wide(np.exp(lgf[:, None] * (c - 1.0 - pos)), RET_DK)
    kdb = wide(np.exp(lgb[:, None] * pos), RET_DK)
    cdf = np.broadcast_to(np.exp(lgf * c)[:, None, None], (heads, 1, RET_DV))
    cdb = np.broadcast_to(np.exp(lgb * c)[:, None, None], (heads, 1, RET_DV))
    return tuple(np.ascontiguousarray(t, dtype=np.float32) for t in (mask, qdf, qdb, kdf, kdb, cdf, cdb))


def _ret_core_call(q, k, v, gate, *, ctx_len):
    bsz, t_all, d_qk = q.shape
    heads = d_qk // RET_DK
    c = RET_BLOCK
    tables = _ret_tables(heads)
    per_head = lambda a: pl.BlockSpec((None,) + a.shape[1:], lambda hi, bi: (hi, 0, 0))
    seq = lambda w: pl.BlockSpec((None, t_all, w), lambda hi, bi: (bi, 0, hi))
    return pl.pallas_call(
        functools.partial(_ret_core_kernel, n_chunks=t_all // c, n_ctx_chunks=ctx_len // c),
        grid=(heads, bsz),
        in_specs=[seq(RET_DK), seq(RET_DK), seq(RET_DV), seq(RET_DV), *[per_head(a) for a in tables]],
        out_specs=seq(RET_DV),
        out_shape=jax.ShapeDtypeStruct((bsz, t_all, heads * RET_DV), BF16),
        scratch_shapes=[
            pltpu.VMEM((t_all // c, RET_DK, RET_DV), BF16),
            pltpu.VMEM((RET_DK, RET_DV), F32),
            pltpu.VMEM((RET_DK, RET_DV), F32),
        ],
        compiler_params=_cparams(2),
        name="retention_core",
    )(q, k, v, gate, *tables)


def _rope_tables(seq, ctx_len):
    half = RET_DK // 2
    freqs = ROPE_BASE ** (-np.arange(0, half, 2, dtype=np.float64) / half)
    pos = np.arange(seq)
    ar = (pos // GRID_W).astype(np.float64)[:, None] * freqs
    ac = (pos % GRID_W).astype(np.float64)[:, None] * freqs
    cos = np.concatenate([np.cos(ar), np.cos(ar), np.cos(ac), np.cos(ac)], axis=-1)
    sin = np.concatenate([-np.sin(ar), np.sin(ar), -np.sin(ac), np.sin(ac)], axis=-1)
    cos = np.concatenate([np.ones((ctx_len, RET_DK)), cos], axis=0)
    sin = np.concatenate([np.zeros((ctx_len, RET_DK)), sin], axis=0)
    return cos.astype(np.float32), sin.astype(np.float32)


def _na_proj_epilogue(j, p, tables, outs, *, d_att):
    k_ref, qt_ref, vt_ref = outs
    qt_ref[j] = (p[:, :d_att] * (NA_HEAD_DIM ** -0.5 * LOG2E)).T.astype(BF16)
    k_ref[j * TM:(j + 1) * TM, :] = p[:, d_att:2 * d_att].astype(BF16)
    vt_ref[j] = p[:, 2 * d_att:].T.astype(BF16)


def _na_proj_call(h, mods, w_qkv, **layer_args):
    d_att = w_qkv.shape[-1] // 3
    return _proj_call("na_proj", functools.partial(_na_proj_epilogue, d_att=d_att), h, mods, w_qkv, [],
                      [("rows", d_att, BF16), ("tiles", d_att, BF16), ("tiles", d_att, BF16)],
                      **layer_args)


def _na_attn_kernel(k_ref, qt_ref, vt_ref, bias_ref, ot_ref, s0_s, s1_s, *, ctx_len, n_blocks):
    hd = NA_HEAD_DIM
    n_ctx_tiles = ctx_len // TM
    row = lax.broadcasted_iota(jnp.int32, (2 * hd, 2 * TM), 0)
    lane = lax.broadcasted_iota(jnp.int32, (2 * hd, 2 * TM), 1)
    own_head = (row < hd) == (lane < TM)
    k_ctx = k_ref[0:ctx_len, :]
    ones = jnp.ones((BF16_SUBLANES, TM), BF16)

    def pv(tile, p_tile):
        return _dot(jnp.concatenate([vt_ref[tile], ones], axis=0), p_tile.astype(BF16))

    def queries(q_tile):
        qt2 = qt_ref[q_tile]
        qt = jnp.concatenate([qt2, qt2], axis=1)
        return jnp.where(own_head, qt, jnp.zeros_like(qt))

    def probs(s, m):
        return jnp.exp2((s - m).astype(BF16))

    def write_out(q_tile, acc):
        o0 = acc[0:hd, 0:TM] * (1.0 / acc[2 * hd:2 * hd + 1, 0:TM])
        o1 = acc[hd:2 * hd, TM:2 * TM] * (1.0 / acc[2 * hd:2 * hd + 1, TM:2 * TM])
        ot_ref[q_tile] = jnp.concatenate([o0, o1], axis=0)

    for t in range(n_ctx_tiles):
        s_ctx = _dot(k_ctx, queries(t))
        p_ctx = probs(s_ctx, jnp.max(s_ctx, axis=0, keepdims=True))
        acc = pv(0, p_ctx[0:TM])
        for j in range(1, n_ctx_tiles):
            acc = acc + pv(j, p_ctx[j * TM:(j + 1) * TM])
        write_out(t, acc)

    def key_tile(blk):
        return n_ctx_tiles + jnp.clip(blk - 1, 0, n_blocks - NA_WIN_TILES)

    def scores(blk, s_s):
        qt = queries(n_ctx_tiles + blk)
        variant = jnp.where(blk == 0, 0, jnp.where(blk == n_blocks - 1, 2, 1))
        k_off = pl.multiple_of(key_tile(blk) * TM, TM)
        s_ctx = _dot(k_ctx, qt)
        s_loc = _dot(k_ref[pl.ds(k_off, NA_WIN_TILES * TM), :], qt)
        s_loc = jnp.concatenate([s_loc[:, :TM] + bias_ref[0, variant],
                                 s_loc[:, TM:] + bias_ref[1, variant]], axis=1)
        s_s[0:ctx_len, :] = s_ctx
        s_s[ctx_len:ctx_len + NA_WIN_TILES * TM, :] = s_loc
        return jnp.maximum(jnp.max(s_ctx, axis=0, keepdims=True), jnp.max(s_loc, axis=0, keepdims=True))

    def output(blk, s_s, m):
        k_tile = key_tile(blk)
        acc = None
        for j in range(n_ctx_tiles + NA_WIN_TILES):
            p = probs(s_s[j * TM:(j + 1) * TM, :], m)
            part = pv(j if j < n_ctx_tiles else k_tile + (j - n_ctx_tiles), p)
            acc = part if acc is None else acc + part
        write_out(n_ctx_tiles + blk, acc)

    def step(j, m_even):
        m_odd = scores(2 * j + 1, s1_s)
        output(2 * j, s0_s, m_even)
        m_even = scores(2 * j + 2, s0_s)
        output(2 * j + 1, s1_s, m_odd)
        return m_even

    m_even = lax.fori_loop(0, n_blocks // 2 - 1, step, scores(0, s0_s))
    m_odd = scores(n_blocks - 1, s1_s)
    output(n_blocks - 2, s0_s, m_even)
    output(n_blocks - 1, s1_s, m_odd)


def _na_bias_tables(rpb, grid_rows):
    heads = rpb.shape[0]
    w = GRID_W
    col = jnp.arange(w)
    start = jnp.clip(col - NA_WIN_C // 2, 0, w - NA_WIN_C)
    ok = (col[:, None] >= start[None, :]) & (col[:, None] < start[None, :] + NA_WIN_C)
    rel_c = jnp.clip(col[:, None] - col[None, :] + NA_WIN_C - 1, 0, 2 * NA_WIN_C - 2)
    colb = jnp.where(ok[None, None], rpb[:, :, rel_c].astype(F32) * LOG2E, NEG_INF)
    masked = jnp.full((heads, w, w), NEG_INF, F32)
    key_rows = NA_WIN_TILES * TM // w

    def build(q_row0, k_row0):
        out = []
        for kj in range(key_rows):
            blocks = []
            for qi in range(NA_ROWS_PER_BLOCK):
                r, kr = q_row0 + qi, k_row0 + kj
                rs = min(max(r - NA_WIN_R // 2, 0), grid_rows - NA_WIN_R)
                inside = rs <= kr < rs + NA_WIN_R
                blocks.append(colb[:, kr - r + NA_WIN_R - 1] if inside else masked)
            out.append(jnp.concatenate(blocks, axis=2))
        return jnp.concatenate(out, axis=1)

    rpb_blk = NA_ROWS_PER_BLOCK
    return jnp.stack([build(0, 0), build(rpb_blk, 0),
                      build(grid_rows - rpb_blk, grid_rows - key_rows)], axis=1)


def _na_attn_call(k, qt, vt, rpb, *, ctx_len):
    bsz, t_all, d_att = k.shape
    n_tiles = t_all // TM
    pairs = d_att // (2 * NA_HEAD_DIM)
    n_blocks = (t_all - ctx_len) // TM
    assert n_blocks >= NA_WIN_TILES and n_blocks % 2 == 0
    bias = _na_bias_tables(rpb, (t_all - ctx_len) // GRID_W)
    t_spec = pl.BlockSpec((None, n_tiles, 2 * NA_HEAD_DIM, TM), lambda pi, bi: (bi, 0, pi, 0))
    return pl.pallas_call(
        functools.partial(_na_attn_kernel, ctx_len=ctx_len, n_blocks=n_blocks),
        grid=(pairs, bsz),
        in_specs=[
            pl.BlockSpec((None, t_all, 2 * NA_HEAD_DIM), lambda pi, bi: (bi, 0, pi)),
            t_spec, t_spec,
            pl.BlockSpec((2,) + bias.shape[1:], lambda pi, bi: (pi, 0, 0, 0)),
        ],
        out_specs=t_spec,
        out_shape=jax.ShapeDtypeStruct((bsz, n_tiles, d_att, TM), F32),
        scratch_shapes=[pltpu.VMEM((ctx_len + NA_WIN_TILES * TM, 2 * TM), F32)] * 2,
        compiler_params=_cparams(2),
        name="na_attention",
    )(k, qt, vt, bias)


def _lru_proj_epilogue(j, p, tables, outs, *, width):
    gate_ref, x_ref = outs
    rows = slice(j * TM, (j + 1) * TM)
    gate_ref[rows, :] = _gelu_tanh(p[:, :width]).astype(BF16)
    x_ref[rows, :] = p[:, width:]


def _lru_proj_call(h, mods, w_in, **layer_args):
    width = w_in.shape[-1] // 2
    return _proj_call("rglru_proj", functools.partial(_lru_proj_epilogue, width=width), h, mods, w_in, [],
                      [("rows", width, BF16), ("rows", width, F32)], **layer_args)


def _lru_core_kernel(x_ref, gate_ref, cw_ref, cb_ref, wa_ref, ba_ref, wx_ref, bx_ref, lam_ref, o_ref,
                     xp_s, a_s, u_s, *, t_all, ctx_len):
    pad = SUBLANES
    ch = LRU_CHUNK
    bw = LRU_BLOCK_W
    n_chunks = t_all // ch
    n_ctx_chunks = ctx_len // ch

    zeros = jnp.zeros((pad, bw), F32)
    xp_s[0:pad, :] = zeros
    xp_s[pad + ctx_len:2 * pad + ctx_len, :] = zeros
    xp_s[2 * pad + t_all:3 * pad + t_all, :] = zeros

    def padded_row(i):
        return pl.multiple_of(i * ch + jnp.where(i >= n_ctx_chunks, 2 * pad, pad), SUBLANES)

    def copy(i, carry):
        r = pl.multiple_of(i * ch, ch)
        xp_s[pl.ds(padded_row(i), ch), :] = x_ref[pl.ds(r, ch), :]
        return carry

    lax.fori_loop(0, n_chunks, copy, 0)

    cw = cw_ref[...]
    cb = cb_ref[...]
    neg_lam = -lam_ref[...]
    softplus = jnp.maximum(neg_lam, 0.0) + jnp.log1p(jnp.exp(-jnp.abs(neg_lam)))
    half_rate = (-0.5 * LRU_C) * softplus

    def gates(i, carry):
        r = pl.multiple_of(i * ch, ch)
        lo = pl.multiple_of(padded_row(i) - pad, SUBLANES)
        xe = xp_s[pl.ds(lo, ch + 2 * pad), :]
        mid = slice(pad, pad + ch)
        x_m2 = pltpu.roll(xe, 2, 0)[mid]
        x_m1 = pltpu.roll(xe, 1, 0)[mid]
        x_p1 = pltpu.roll(xe, ch + 2 * pad - 1, 0)[mid]
        x = cw[0:1] * x_m2 + cw[1:2] * x_m1 + cw[2:3] * xe[mid] + cw[3:4] * x_p1 + cb
        xb = x.astype(BF16)
        x_half = 0.5 * x
        for d in range(2):
            tr = jnp.tanh(_dot(xb, wa_ref[d]) + ba_ref[d:d + 1])
            ti = jnp.tanh(_dot(xb, wx_ref[d]) + bx_ref[d:d + 1])
            log_a = tr * half_rate[d:d + 1] + half_rate[d:d + 1]
            a_s[d, pl.ds(r, ch), :] = jnp.exp(log_a)
            th = jnp.tanh(log_a)
            m2 = -2.0 * th / (1.0 - th)
            mult = m2 * lax.rsqrt(jnp.maximum(m2, 1e-37))
            u_s[d, pl.ds(r, ch), :] = mult * (ti * x_half + x_half)
        return carry

    lax.fori_loop(0, n_chunks, gates, 0)

    sub = lax.broadcasted_iota(jnp.int32, (SUBLANES, bw), 0)
    rows_per_iter = SUBLANES * SCAN_GROUPS
    n_blocks = t_all // rows_per_iter
    n_ctx_blocks = ctx_len // rows_per_iter

    def scan8(a, u, reverse):
        for s in (1, 2, 4):
            if reverse:
                valid = sub < SUBLANES - s
                shift = SUBLANES - s
            else:
                valid = sub >= s
                shift = s
            u = u + a * jnp.where(valid, pltpu.roll(u, shift, 0), 0.0)
            a = a * jnp.where(valid, pltpu.roll(a, shift, 0), 1.0)
        return a, u

    def rec(i, carry):
        hf, hb = carry
        rf = pl.multiple_of(i * rows_per_iter, rows_per_iter)
        bb = jnp.where(i < n_ctx_blocks, n_ctx_blocks - 1 - i, n_blocks - 1 - (i - n_ctx_blocks))
        rb = pl.multiple_of(bb * rows_per_iter, rows_per_iter)
        af, uf = a_s[0, pl.ds(rf, rows_per_iter), :], u_s[0, pl.ds(rf, rows_per_iter), :]
        ab, ub = a_s[1, pl.ds(rb, rows_per_iter), :], u_s[1, pl.ds(rb, rows_per_iter), :]
        outs_f, outs_b = [], []
        for j in range(SCAN_GROUPS):
            g = slice(j * SUBLANES, (j + 1) * SUBLANES)
            a, u = scan8(af[g], uf[g], False)
            h = u + a * hf
            outs_f.append(h)
            hf = h[SUBLANES - 1:SUBLANES]
            g = slice((SCAN_GROUPS - 1 - j) * SUBLANES, (SCAN_GROUPS - j) * SUBLANES)
            a, u = scan8(ab[g], ub[g], True)
            h = u + a * hb
            outs_b.append(h)
            hb = h[0:1]
        u_s[0, pl.ds(rf, rows_per_iter), :] = jnp.concatenate(outs_f, axis=0)
        u_s[1, pl.ds(rb, rows_per_iter), :] = jnp.concatenate(outs_b[::-1], axis=0)
        return hf, hb

    zero = jnp.zeros((1, bw), F32)
    lax.fori_loop(0, n_blocks, rec, (zero, zero))

    def emit(i, carry):
        r = pl.multiple_of(i * ch, ch)
        hsum = u_s[0, pl.ds(r, ch), :] + u_s[1, pl.ds(r, ch), :]
        o_ref[pl.ds(r, ch), :] = (hsum * gate_ref[pl.ds(r, ch), :]).astype(BF16)
        return carry

    lax.fori_loop(0, n_chunks, emit, 0)


def _lru_core_call(x, gate, conv_w, conv_b, w_a, b_a, w_x, b_x, lam, *, idx, ctx_len):
    bsz, t_all, width = x.shape
    bw = LRU_BLOCK_W
    blocks = width // bw
    col = lambda rows: pl.BlockSpec((None, rows, bw), lambda ki, bi: (idx, 0, ki))
    wspec = pl.BlockSpec((None, 2, None, bw, bw), lambda ki, bi: (idx, 0, ki, 0, 0))
    seq = pl.BlockSpec((None, t_all, bw), lambda ki, bi: (bi, 0, ki))
    return pl.pallas_call(
        functools.partial(_lru_core_kernel, t_all=t_all, ctx_len=ctx_len),
        grid=(blocks, bsz),
        in_specs=[seq, seq, col(LRU_CONV_W), col(1), wspec, col(2), wspec, col(2), col(2)],
        out_specs=seq,
        out_shape=jax.ShapeDtypeStruct((bsz, t_all, width), BF16),
        scratch_shapes=[
            pltpu.VMEM((t_all + 3 * SUBLANES, bw), F32),
            pltpu.VMEM((2, t_all, bw), F32),
            pltpu.VMEM((2, t_all, bw), F32),
        ],
        compiler_params=_cparams(2),
        name="rglru_core",
    )(x, gate, conv_w, conv_b, w_a, b_a, w_x, b_x, lam)


def kernel(x, c, ctx, c_ctx, ada_w, ada_b, ln_g, ln_b, ffn_w_in, ffn_w_out, ret_w_in, ret_w_out,
           na_w_qkv, na_rpb, na_w_out, lru_w_in, lru_conv_w, lru_conv_b, lru_w_a, lru_b_a,
           lru_w_x, lru_b_x, lru_lam, lru_w_out):
    bsz, seq, d = x.shape
    ctx_len = ctx.shape[1]
    t_all = ctx_len + seq
    depth = ada_w.shape[0]
    alpha = (2 * depth) ** 0.25
    assert ctx_len % TM == 0 and seq % TM == 0 and ctx_len % LRU_CHUNK == 0
    assert ctx_len % RET_BLOCK == 0 and seq % RET_BLOCK == 0
    assert NA_ROWS_PER_BLOCK * GRID_W == TM
    assert (NA_WIN_R + NA_ROWS_PER_BLOCK) * GRID_W == NA_WIN_TILES * TM
    n_ctx_tiles = ctx_len // TM

    s = jnp.concatenate([jnp.broadcast_to(c_ctx[None, :], (bsz, d)), c], axis=0)
    mods = _mods_call(s, ada_w, ada_b).reshape(depth, 2, bsz, N_MOD, d)
    cos, sin = _rope_tables(seq, ctx_len)
    bf = lambda w: w.astype(BF16)
    ffn_w_in, ffn_w_out = bf(ffn_w_in), bf(ffn_w_out)
    ret_w_in, ret_w_out = bf(ret_w_in), bf(ret_w_out)
    na_w_qkv, na_w_out = bf(na_w_qkv), bf(na_w_out)
    lru_w_in, lru_w_out = bf(lru_w_in), bf(lru_w_out)
    lru_w_a, lru_w_x, lru_b_a, lru_b_x = bf(0.5 * lru_w_a), bf(0.5 * lru_w_x), 0.5 * lru_b_a, 0.5 * lru_b_x
    ln_g = ln_g.reshape(depth, 3, 1, d)
    ln_b = ln_b.reshape(depth, 3, 1, d)
    lru_conv_b = lru_conv_b[:, None, :]

    h = None
    for layer in range(depth):
        last = layer == depth - 1
        kind, idx = layer % 3, layer // 3
        lay = dict(layer=layer, idx=idx, n_ctx_tiles=n_ctx_tiles)
        h = _ffn_in_call([ctx, x] if layer == 0 else [h], mods, ffn_w_in, ffn_w_out, ln_g, ln_b,
                         layer=layer, alpha=alpha, n_ctx_tiles=n_ctx_tiles, t_all=t_all)
        if kind == 0:
            q, k, v, sg = _ret_proj_call(h, mods, ret_w_in, cos, sin, **lay)
            act, w_mix = _ret_core_call(q, k, v, sg, ctx_len=ctx_len), ret_w_out
        elif kind == 1:
            k, qt, vt = _na_proj_call(h, mods, na_w_qkv, **lay)
            act, w_mix = _na_attn_call(k, qt, vt, na_rpb[idx], ctx_len=ctx_len), na_w_out
        else:
            gate, xr = _lru_proj_call(h, mods, lru_w_in, **lay)
            act = _lru_core_call(xr, gate, lru_conv_w, lru_conv_b, lru_w_a, lru_b_a, lru_w_x, lru_b_x,
                                 lru_lam, idx=idx, ctx_len=ctx_len)
            w_mix = lru_w_out
        h = _post_call(h, mods, act, w_mix, ffn_w_in, ffn_w_out, ln_g, ln_b, alpha=alpha,
                       lat_only=last, **lay)
    return h
```
